```python
import jax, jax.numpy as jnp
from jax import lax
import numpy as np

D_MODEL = 2048
BATCH = 2
SEQ = 4096
DEPTH = 4
DEC_BATCH = 8
DEC_SEQ = 1
PAST_LEN = 16384
PAGE_SIZE = 128

N_EVEN = (DEPTH + 1) // 2
N_ODD = DEPTH // 2
RMS_EPS = 1e-6
POOL_WINDOWS = (2, 4, 8, 16)
POOL_GROUPS = len(POOL_WINDOWS)
POOL_WIDTH = D_MODEL // 2
POOL_GW = POOL_WIDTH // POOL_GROUPS
POOL_BUF = max(POOL_WINDOWS) - 1
MOBA_HEADS = 8
MOBA_HD = (D_MODEL // 2) // MOBA_HEADS
MOBA_WIDTH = MOBA_HEADS * MOBA_HD
MOBA_BLOCK = 256
MOBA_TOPK = 3
MOBA_Q_CHUNK = 64
EVEN_IN = POOL_WIDTH + 3 * MOBA_WIDTH
GLA_HEADS = 4
GLA_DK_TOT = D_MODEL // 2
GLA_DV_TOT = D_MODEL
GLA_DK = GLA_DK_TOT // GLA_HEADS
GLA_DV = GLA_DV_TOT // GLA_HEADS
GLA_GATE_RANK = 16
GLA_GATE_TAU = 16.0
GLA_CHUNK = 64
ODD_IN = 2 * GLA_DK_TOT + 2 * GLA_DV_TOT + GLA_GATE_RANK
D_FF = ((8 * D_MODEL + 3 * 256 - 1) // (3 * 256)) * 256
PLE_DIM = 256

kernel_name = 'hybrid_pool_moba_gla_decoder_step'


def rmsnorm(x, g):
    xf = x.astype(jnp.float32)
    y = xf * lax.rsqrt(jnp.mean(xf * xf, axis=-1, keepdims=True) + RMS_EPS)
    return (y * g.astype(jnp.float32)).astype(x.dtype)


def alibi_slopes(n):
    return 2.0 ** (-8.0 * (jnp.arange(n, dtype=jnp.float32) + 1.0) / n)


def pool_mixer(u, buf, start, w_pool, scale):
    B, T, C = u.shape
    ext = jnp.concatenate([buf.astype(jnp.float32), u.astype(jnp.float32)], axis=1)
    cs = jnp.concatenate([jnp.zeros((B, 1, C), jnp.float32), jnp.cumsum(ext, axis=1)], axis=1)
    hi = cs[:, POOL_BUF + 1:]
    pos = start + jnp.arange(T, dtype=jnp.int32)
    means = []
    for g, w in enumerate(POOL_WINDOWS):
        c0, c1 = g * POOL_GW, (g + 1) * POOL_GW
        lo = cs[:, POOL_BUF + 1 - w: POOL_BUF + 1 - w + T, c0:c1]
        cnt = jnp.minimum(pos + 1, w).astype(jnp.float32)[None, :, None]
        means.append((hi[..., c0:c1] - lo) / cnt)
    d = (jnp.concatenate(means, axis=-1) - ext[:, POOL_BUF:]).reshape(B, T, POOL_GROUPS, POOL_GW)
    y = jnp.einsum('btgc,gcd->btgd', d, w_pool.astype(jnp.float32)).reshape(B, T, C)
    y = y * scale.astype(jnp.float32)
    return y.astype(u.dtype), ext[:, -POOL_BUF:].astype(u.dtype)


def moba_attention(q, k, v, start):
    B, T, H, HD = q.shape
    L = k.shape[1]
    nb = max(-(-L // MOBA_BLOCK), MOBA_TOPK)
    pad = nb * MOBA_BLOCK - L
    kb = jnp.pad(k, ((0, 0), (0, pad), (0, 0), (0, 0))).reshape(B, nb, MOBA_BLOCK, H, HD)
    vb = jnp.pad(v, ((0, 0), (0, pad), (0, 0), (0, 0))).reshape(B, nb, MOBA_BLOCK, H, HD)
    kmean = jnp.mean(kb.astype(jnp.float32), axis=2)
    slopes = alibi_slopes(H)
    scale = HD ** -0.5
    qc = MOBA_Q_CHUNK if T % MOBA_Q_CHUNK == 0 else T
    n_chunks = T // qc
    qch = q.reshape(B, n_chunks, qc, H, HD).transpose(1, 0, 3, 2, 4)
    pos = (start + jnp.arange(T, dtype=jnp.int32)).reshape(n_chunks, qc)
    bi = jnp.arange(B)[:, None, None, None]
    hi = jnp.arange(H)[None, :, None, None]
    blk_ids = jnp.arange(nb, dtype=jnp.int32)
    offs = jnp.arange(MOBA_BLOCK, dtype=jnp.int32)

    def one_chunk(args):
        qq, pp = args
        qf = qq.astype(jnp.float32)
        own = pp // MOBA_BLOCK
        gate = jnp.einsum('bhqd,bnhd->bhqn', qf, kmean)
        fully_past = blk_ids[None, :] < own[:, None]
        gate = jnp.where(fully_past[None, None], gate, -jnp.inf)
        _, top = lax.top_k(gate, MOBA_TOPK)
        top_valid = jnp.arange(MOBA_TOPK)[None, :] < own[:, None]
        sel = jnp.concatenate([top, jnp.broadcast_to(own[None, None, :, None], (B, H, qc, 1))], axis=-1)
        sel_valid = jnp.concatenate([top_valid, jnp.ones((qc, 1), bool)], axis=-1)
        kg = kb[bi, sel, :, hi].astype(jnp.float32)
        vg = vb[bi, sel, :, hi].astype(jnp.float32)
        kpos = sel[..., None] * MOBA_BLOCK + offs
        s = jnp.einsum('bhqd,bhqnkd->bhqnk', qf, kg) * scale
        s = s - slopes[None, :, None, None, None] * (pp[:, None, None] - kpos).astype(jnp.float32)
        mask = sel_valid[None, None, :, :, None] & (kpos <= pp[:, None, None])
        s = jnp.where(mask, s, -jnp.inf)
        pr = jax.nn.softmax(s.reshape(B, H, qc, -1), axis=-1).reshape(s.shape)
        o = jnp.einsum('bhqnk,bhqnkd->bhqd', pr, vg)
        return o.astype(q.dtype)

    o = lax.map(one_chunk, (qch, pos))
    return o.transpose(1, 0, 3, 2, 4).reshape(B, T, H * HD)


def gla_chunked(q, k, v, lg, s0):
    B, T, H, DK = q.shape
    C = GLA_CHUNK if T % GLA_CHUNK == 0 else T
    N = T // C

    def chunks(a):
        return a.astype(jnp.float32).reshape(B, N, C, H, a.shape[-1]).transpose(1, 0, 3, 2, 4)

    qc, kc, vc, gc = [chunks(a) for a in (q * DK ** -0.5, k, v, lg)]
    causal = jnp.tril(jnp.ones((C, C), bool))

    def step(S, inp):
        qq, kk, vv, gg = inp
        b = jnp.cumsum(gg, axis=2)
        q_dec = qq * jnp.exp(b)
        A = jnp.einsum('bhtd,bhsd->bhts', q_dec, kk * jnp.exp(-b))
        A = jnp.where(causal, A, 0.0)
        o = jnp.einsum('bhts,bhsv->bhtv', A, vv) + jnp.einsum('bhtd,bhdv->bhtv', q_dec, S)
        b_last = b[:, :, -1:, :]
        S_new = jnp.exp(b_last[:, :, 0, :])[..., None] * S + jnp.einsum('bhsd,bhsv->bhdv', kk * jnp.exp(b_last - b), vv)
        return S_new, o

    S, o = lax.scan(step, s0.astype(jnp.float32), (qc, kc, vc, gc))
    return o.transpose(1, 0, 3, 2, 4).reshape(B, T, H, v.shape[-1]), S


def paged_rows(pool, layer, page_table):
    g = pool[layer, page_table]
    return g.reshape(page_table.shape[0], -1, pool.shape[3], pool.shape[4])


def trunk(x, p, start, pool_state, gla_state, cache_k, cache_v, page_table,
          g_mix, g_ffn, g_ple, g_final, w_in_even, w_pool, pool_scale, w_out_even,
          w_in_odd, w_gk2, b_gk, g_gla, w_out_odd, w_ffn_gate, w_ffn_up, w_ffn_down,
          w_ple_gate, w_ple_proj):
    B, T, _ = x.shape
    h = x
    pools, ks, vs, glas = [], [], [], []
    for i in range(DEPTH):
        j = i // 2
        hn = rmsnorm(h, g_mix[i])
        if i % 2 == 0:
            z = hn @ w_in_even[j]
            u_a, q, k, v = jnp.split(z, [POOL_WIDTH, POOL_WIDTH + MOBA_WIDTH, POOL_WIDTH + 2 * MOBA_WIDTH], axis=-1)
            y_a, buf = pool_mixer(u_a, pool_state[j], start, w_pool[j], pool_scale[j])
            hs = (B, T, MOBA_HEADS, MOBA_HD)
            q, k, v = q.reshape(hs), k.reshape(hs), v.reshape(hs)
            if cache_k is None:
                k_all, v_all = k, v
            else:
                k_all = jnp.concatenate([paged_rows(cache_k, j, page_table), k], axis=1)
                v_all = jnp.concatenate([paged_rows(cache_v, j, page_table), v], axis=1)
            y_b = moba_attention(q, k_all, v_all, start)
            mix = jnp.concatenate([y_a, y_b], axis=-1) @ w_out_even[j]
            pools.append(buf)
            ks.append(k)
            vs.append(v)
        else:
            z = hn @ w_in_odd[j]
            q, k, v, r, a = jnp.split(z, [GLA_DK_TOT, 2 * GLA_DK_TOT, 2 * GLA_DK_TOT + GLA_DV_TOT,
                                          2 * GLA_DK_TOT + 2 * GLA_DV_TOT], axis=-1)
            lg = jax.nn.log_sigmoid((a @ w_gk2[j] + b_gk[j]).astype(jnp.float32)) / GLA_GATE_TAU
            kshape = (B, T, GLA_HEADS, GLA_DK)
            o, s_new = gla_chunked(q.reshape(kshape), k.reshape(kshape), v.reshape(B, T, GLA_HEADS, GLA_DV),
                                   lg.reshape(kshape), gla_state[j])
            o = rmsnorm(o.astype(x.dtype), g_gla[j]).reshape(B, T, GLA_DV_TOT)
            mix = (o * jax.nn.silu(r)) @ w_out_odd[j]
            glas.append(s_new.astype(gla_state.dtype))
        h = h + mix
        hn = rmsnorm(h, g_ffn[i])
        h = h + (jax.nn.silu(hn @ w_ffn_gate[i]) * (hn @ w_ffn_up[i])) @ w_ffn_down[i]
        hn = rmsnorm(h, g_ple[i])
        h = h + jax.nn.sigmoid(hn @ w_ple_gate[i]) * (p[i] @ w_ple_proj[i])
    y = rmsnorm(h, g_final)
    return y, jnp.stack(pools), jnp.stack(ks), jnp.stack(vs), jnp.stack(glas)


def setup_inputs(seed: int = 0) -> dict:
    key = jax.random.key(seed)
    keys = iter(jax.random.split(key, 40))
    f32 = jnp.float32

    def nrm(shape, scale=1.0):
        return jax.random.normal(next(keys), shape, f32) * scale

    n_pages = PAST_LEN // PAGE_SIZE
    n_pool_pages = (5 * DEC_BATCH * n_pages + 3) // 4
    perm = jax.random.permutation(next(keys), n_pool_pages)
    page_table = perm[:DEC_BATCH * n_pages].reshape(DEC_BATCH, n_pages).astype(jnp.int32)
    return {
        'x_prompt': nrm((BATCH, SEQ, D_MODEL)),
        'x_sample': nrm((DEC_BATCH, DEC_SEQ, D_MODEL)),
        'state_pool': nrm((N_EVEN, DEC_BATCH, POOL_BUF, POOL_WIDTH)),
        'cache_k': nrm((N_EVEN, n_pool_pages, PAGE_SIZE, MOBA_HEADS, MOBA_HD)),
        'cache_v': nrm((N_EVEN, n_pool_pages, PAGE_SIZE, MOBA_HEADS, MOBA_HD)),
        'state_gla': nrm((N_ODD, DEC_BATCH, GLA_HEADS, GLA_DK, GLA_DV)),
        'page_table': page_table,
        'p_prompt': nrm((DEPTH, BATCH, SEQ, PLE_DIM)),
        'p_sample': nrm((DEPTH, DEC_BATCH, DEC_SEQ, PLE_DIM)),
        'g_mix': 1.0 + nrm((DEPTH, D_MODEL), 0.02),
        'g_ffn': 1.0 + nrm((DEPTH, D_MODEL), 0.02),
        'g_ple': 1.0 + nrm((DEPTH, D_MODEL), 0.02),
        'g_final': 1.0 + nrm((D_MODEL,), 0.02),
        'w_in_even': nrm((N_EVEN, D_MODEL, EVEN_IN), D_MODEL ** -0.5),
        'w_pool': nrm((N_EVEN, POOL_GROUPS, POOL_GW, POOL_GW), POOL_GW ** -0.5),
        'pool_scale': 1.0 + nrm((N_EVEN, POOL_WIDTH), 0.02),
        'w_out_even': nrm((N_EVEN, POOL_WIDTH + MOBA_WIDTH, D_MODEL), (POOL_WIDTH + MOBA_WIDTH) ** -0.5),
        'w_in_odd': nrm((N_ODD, D_MODEL, ODD_IN), D_MODEL ** -0.5),
        'w_gk2': nrm((N_ODD, GLA_GATE_RANK, GLA_DK_TOT), GLA_GATE_RANK ** -0.5),
        'b_gk': nrm((N_ODD, GLA_DK_TOT), 0.1),
        'g_gla': 1.0 + nrm((N_ODD, GLA_DV), 0.02),
        'w_out_odd': nrm((N_ODD, GLA_DV_TOT, D_MODEL), GLA_DV_TOT ** -0.5),
        'w_ffn_gate': nrm((DEPTH, D_MODEL, D_FF), D_MODEL ** -0.5),
        'w_ffn_up': nrm((DEPTH, D_MODEL, D_FF), D_MODEL ** -0.5),
        'w_ffn_down': nrm((DEPTH, D_FF, D_MODEL), D_FF ** -0.5),
        'w_ple_gate': nrm((DEPTH, D_MODEL, D_MODEL), D_MODEL ** -0.5),
        'w_ple_proj': nrm((DEPTH, PLE_DIM, D_MODEL), PLE_DIM ** -0.5),
    }


def reference(x_prompt, x_sample, state_pool, cache_k, cache_v, state_gla, page_table, p_prompt, p_sample,
              g_mix, g_ffn, g_ple, g_final, w_in_even, w_pool, pool_scale, w_out_even,
              w_in_odd, w_gk2, b_gk, g_gla, w_out_odd, w_ffn_gate, w_ffn_up, w_ffn_down,
              w_ple_gate, w_ple_proj):
    weights = (g_mix, g_ffn, g_ple, g_final, w_in_even, w_pool, pool_scale, w_out_even,
               w_in_odd, w_gk2, b_gk, g_gla, w_out_odd, w_ffn_gate, w_ffn_up, w_ffn_down,
               w_ple_gate, w_ple_proj)
    b = x_prompt.shape[0]
    pool0 = jnp.zeros((N_EVEN, b, POOL_BUF, POOL_WIDTH), x_prompt.dtype)
    gla0 = jnp.zeros((N_ODD, b, GLA_HEADS, GLA_DK, GLA_DV), state_gla.dtype)
    y_prompt, pool_p, k_p, v_p, gla_p = trunk(x_prompt, p_prompt, 0, pool0, gla0, None, None, None, *weights)
    past_len = page_table.shape[1] * cache_k.shape[2]
    y_sample, pool_s, k_s, v_s, gla_s = trunk(x_sample, p_sample, past_len, state_pool, state_gla,
                                              cache_k, cache_v, page_table, *weights)
    return (y_prompt, y_sample, pool_p, pool_s, k_p, k_s, v_p, v_s, gla_p, gla_s)
```

```python
import functools

import jax
import jax.numpy as jnp
from jax import lax
from jax.experimental import pallas as pl
from jax.experimental.pallas import tpu as pltpu

F32 = jnp.float32
BF16 = jnp.bfloat16

RMS_EPS = 1e-6
POOL_WINDOWS = (2, 4, 8, 16)
POOL_HALO = 16
MOBA_BLOCK = 256
MOBA_TOPK = 3
GLA_GATE_RANK = 16
GLA_GATE_TAU = 16.0
GLA_CHUNK = 64
LANES = 128
VMEM_LIMIT = 52 * 1024 * 1024
NEG_INF = float("-inf")


def _cparams(*sem):
    return pltpu.CompilerParams(dimension_semantics=sem, vmem_limit_bytes=VMEM_LIMIT)


def _rmsnorm(x, g):
    ms = jnp.mean(x * x, axis=-1, keepdims=True)
    return x * lax.rsqrt(ms + RMS_EPS) * g


def _dot(a, b):
    return jnp.dot(a, b, preferred_element_type=F32)


def _dot_nt(a, b):
    return lax.dot_general(a, b, (((1,), (1,)), ((), ())), preferred_element_type=F32)


def _dot_tn(a, b):
    return lax.dot_general(a, b, (((0,), (0,)), ((), ())), preferred_element_type=F32)


def _split2(a):
    hi = a.astype(BF16)
    lo = (a - hi.astype(F32)).astype(BF16)
    return hi, lo


def _split3(a):
    hi = a.astype(BF16)
    r = a - hi.astype(F32)
    mid = r.astype(BF16)
    lo = (r - mid.astype(F32)).astype(BF16)
    return hi, mid, lo


def _dot_nt_3pass(a, b):
    ah, al = _split2(a)
    bh, bl = _split2(b)
    return _dot_nt(ah, bh) + (_dot_nt(ah, bl) + _dot_nt(al, bh))


def _tile(m, t):
    t = min(m, t)
    assert m % t == 0, (m, t)
    return t


def _act_dtype(tm):
    return BF16 if tm % 16 == 0 else F32


def _norm_matmul_kernel(x_ref, g_ref, w_ref, o_ref, hn_ref):
    @pl.when(pl.program_id(1) == 0)
    def _():
        hn_ref[...] = _rmsnorm(x_ref[...], g_ref[...]).astype(hn_ref.dtype)

    o_ref[...] = _dot(hn_ref[...].astype(BF16), w_ref[...])


def _norm_matmul(x, g, w, *, tm=1024, tn=512):
    m, d = x.shape
    n = w.shape[1]
    tm, tn = _tile(m, tm), _tile(n, tn)
    return pl.pallas_call(
        _norm_matmul_kernel,
        grid=(m // tm, n // tn),
        in_specs=[pl.BlockSpec((tm, d), lambda i, j: (i, 0)),
                  pl.BlockSpec((1, d), lambda i, j: (0, 0)),
                  pl.BlockSpec((d, tn), lambda i, j: (0, j))],
        out_specs=pl.BlockSpec((tm, tn), lambda i, j: (i, j)),
        out_shape=jax.ShapeDtypeStruct((m, n), F32),
        scratch_shapes=[pltpu.VMEM((tm, d), _act_dtype(tm))],
        compiler_params=_cparams("parallel", "arbitrary"),
        name="norm_matmul",
    )(x, g.reshape(1, d), w)


def _proj_res_kernel(*refs, n_in):
    a_refs, w_refs = refs[:n_in], refs[n_in:2 * n_in]
    res_ref, o_ref = refs[2 * n_in], refs[2 * n_in + 1]
    s_refs = refs[2 * n_in + 2:]

    @pl.when(pl.program_id(1) == 0)
    def _():
        for a_ref, s_ref in zip(a_refs, s_refs):
            s_ref[...] = a_ref[...].astype(s_ref.dtype)

    acc = res_ref[...]
    for s_ref, w_ref in zip(s_refs, w_refs):
        acc = acc + _dot(s_ref[...].astype(BF16), w_ref[...])
    o_ref[...] = acc


def _proj_res(a_list, w_list, res, *, tm=1024, tn=512):
    m, n = res.shape
    tm, tn = _tile(m, tm), _tile(n, tn)
    n_in = len(a_list)
    in_specs = ([pl.BlockSpec((tm, a.shape[1]), lambda i, j: (i, 0)) for a in a_list]
                + [pl.BlockSpec((w.shape[0], tn), lambda i, j: (0, j)) for w in w_list]
                + [pl.BlockSpec((tm, tn), lambda i, j: (i, j))])
    return pl.pallas_call(
        functools.partial(_proj_res_kernel, n_in=n_in),
        grid=(m // tm, n // tn),
        in_specs=in_specs,
        out_specs=pl.BlockSpec((tm, tn), lambda i, j: (i, j)),
        out_shape=jax.ShapeDtypeStruct((m, n), F32),
        scratch_shapes=[pltpu.VMEM((tm, a.shape[1]), _act_dtype(tm)) for a in a_list],
        compiler_params=_cparams("parallel", "arbitrary"),
        name="proj_res",
    )(*a_list, *w_list, res)


def _ffn_kernel(x_ref, g_ref, wg_ref, wu_ref, wd_ref, o_ref, hn_ref):
    @pl.when(pl.program_id(1) == 0)
    def _():
        x = x_ref[...]
        hn_ref[...] = _rmsnorm(x, g_ref[...]).astype(hn_ref.dtype)
        o_ref[...] = x

    hn = hn_ref[...].astype(BF16)
    a = _dot(hn, wg_ref[...])
    b = _dot(hn, wu_ref[...])
    h = (a * jax.nn.sigmoid(a)) * b
    o_ref[...] += _dot(h.astype(BF16), wd_ref[...])


def _ffn(x, g, wg, wu, wd, *, tm=512, tf=512):
    m, d = x.shape
    f = wg.shape[1]
    tm, tf = _tile(m, tm), _tile(f, tf)
    return pl.pallas_call(
        _ffn_kernel,
        grid=(m // tm, f // tf),
        in_specs=[pl.BlockSpec((tm, d), lambda i, j: (i, 0)),
                  pl.BlockSpec((1, d), lambda i, j: (0, 0)),
                  pl.BlockSpec((d, tf), lambda i, j: (0, j)),
                  pl.BlockSpec((d, tf), lambda i, j: (0, j)),
                  pl.BlockSpec((tf, d), lambda i, j: (j, 0))],
        out_specs=pl.BlockSpec((tm, d), lambda i, j: (i, 0)),
        out_shape=jax.ShapeDtypeStruct((m, d), F32),
        scratch_shapes=[pltpu.VMEM((tm, d), _act_dtype(tm))],
        compiler_params=_cparams("parallel", "arbitrary"),
        name="ffn",
    )(x, g.reshape(1, d), wg, wu, wd)


def _ple_kernel(x_ref, xj_ref, g_ref, wg_ref, p_ref, wp_ref, o_ref, hn_ref):
    @pl.when(pl.program_id(1) == 0)
    def _():
        hn_ref[...] = _rmsnorm(x_ref[...], g_ref[...]).astype(hn_ref.dtype)

    gate = _dot(hn_ref[...].astype(BF16), wg_ref[...])
    proj = _dot(p_ref[...].astype(BF16), wp_ref[...])
    o_ref[...] = xj_ref[...] + jax.nn.sigmoid(gate) * proj


def _ple(x, g, wg, p, wp, *, tm=1024, tn=512):
    m, d = x.shape
    tm, tn = _tile(m, tm), _tile(d, tn)
    pd = p.shape[1]
    return pl.pallas_call(
        _ple_kernel,
        grid=(m // tm, d // tn),
        in_specs=[pl.BlockSpec((tm, d), lambda i, j: (i, 0)),
                  pl.BlockSpec((tm, tn), lambda i, j: (i, j)),
                  pl.BlockSpec((1, d), lambda i, j: (0, 0)),
                  pl.BlockSpec((d, tn), lambda i, j: (0, j)),
                  pl.BlockSpec((tm, pd), lambda i, j: (i, 0)),
                  pl.BlockSpec((pd, tn), lambda i, j: (0, j))],
        out_specs=pl.BlockSpec((tm, tn), lambda i, j: (i, j)),
        out_shape=jax.ShapeDtypeStruct((m, d), F32),
        scratch_shapes=[pltpu.VMEM((tm, d), _act_dtype(tm))],
        compiler_params=_cparams("parallel", "arbitrary"),
        name="ple",
    )(x, x, g.reshape(1, d), wg, p, wp)


def _final_norm_kernel(x_ref, g_ref, o_ref):
    o_ref[...] = _rmsnorm(x_ref[...], g_ref[...])


def _final_norm(x, g, *, tm=512):
    m, d = x.shape
    tm = _tile(m, tm)
    return pl.pallas_call(
        _final_norm_kernel,
        grid=(m // tm,),
        in_specs=[pl.BlockSpec((tm, d), lambda i: (i, 0)),
                  pl.BlockSpec((1, d), lambda i: (0, 0))],
        out_specs=pl.BlockSpec((tm, d), lambda i: (i, 0)),
        out_shape=jax.ShapeDtypeStruct((m, d), F32),
        compiler_params=_cparams("parallel"),
        name="final_norm",
    )(x, g.reshape(1, d))


def _pool_kernel(u_ref, halo_ref, w_ref, s_ref, o_ref, ext_ref, *, tt, start, gw):
    t = pl.program_id(1)

    @pl.when(t == 0)
    def _():
        ext_ref[0:POOL_HALO, :] = halo_ref[0]

    @pl.when(t > 0)
    def _():
        ext_ref[0:POOL_HALO, :] = ext_ref[tt:tt + POOL_HALO, :]

    ext_ref[POOL_HALO:POOL_HALO + tt, :] = u_ref[0]
    pos = start + t * tt + lax.broadcasted_iota(jnp.int32, (tt, 1), 0)
    for g, w in enumerate(POOL_WINDOWS):
        c0, c1 = g * gw, (g + 1) * gw
        cur = ext_ref[POOL_HALO:POOL_HALO + tt, c0:c1]
        acc = cur
        for k in range(1, w):
            acc = acc + ext_ref[POOL_HALO - k:POOL_HALO - k + tt, c0:c1]
        cnt = jnp.minimum(pos + 1, w).astype(F32)
        d = acc / cnt - cur
        y = _dot(d.astype(BF16), w_ref[g])
        o_ref[0, :, c0:c1] = y * s_ref[:, c0:c1]


def _pool_mixer(z, buf, start, w_pool, scale, *, width, tt=512):
    b, t, _ = z.shape
    tt = _tile(t, tt)
    gw = width // len(POOL_WINDOWS)
    halo = jnp.pad(buf, ((0, 0), (POOL_HALO - buf.shape[1], 0), (0, 0)))
    return pl.pallas_call(
        functools.partial(_pool_kernel, tt=tt, start=start, gw=gw),
        grid=(b, t // tt),
        in_specs=[pl.BlockSpec((1, tt, width), lambda i, j: (i, j, 0)),
                  pl.BlockSpec((1, POOL_HALO, width), lambda i, j: (i, 0, 0)),
                  pl.BlockSpec(w_pool.shape, lambda i, j: (0, 0, 0)),
                  pl.BlockSpec((1, width), lambda i, j: (0, 0))],
        out_specs=pl.BlockSpec((1, tt, width), lambda i, j: (i, j, 0)),
        out_shape=jax.ShapeDtypeStruct((b, t, width), F32),
        scratch_shapes=[pltpu.VMEM((POOL_HALO + max(tt, POOL_HALO), width), F32)],
        compiler_params=_cparams("parallel", "arbitrary"),
        name="pool_mixer",
    )(z, halo, w_pool, scale.reshape(1, width))


def _moba_prompt_kernel(q_ref, k_ref, v_ref, o_ref, kmean_ref, *, nb, hd, heads):
    h = pl.program_id(1)
    qi = pl.program_id(2)
    blk = MOBA_BLOCK

    @pl.when(qi == 0)
    def _():
        kmean_ref[...] = jnp.zeros_like(kmean_ref)
        for n in range(nb):
            ksum = jnp.sum(k_ref[0, n * blk:(n + 1) * blk, :], axis=0, keepdims=True)
            kmean_ref[n:n + 1, :] = ksum / blk

    q = q_ref[0]
    slope = jnp.exp2(-jnp.full((1, 1), h + 1, jnp.int32).astype(F32) * (8.0 / heads))
    scale = hd ** -0.5

    lane = lax.broadcasted_iota(jnp.int32, (blk, LANES), 1)
    fully_past = lane < qi
    gate = jnp.where(fully_past, _dot_nt_3pass(q, kmean_ref[...]), NEG_INF)
    sel = jnp.zeros((blk, LANES), jnp.bool_)
    for _ in range(MOBA_TOPK):
        m = jnp.max(gate, axis=-1, keepdims=True)
        first = jnp.min(jnp.where(gate == m, lane, LANES), axis=-1, keepdims=True)
        pick = lane == first
        sel = sel | pick
        gate = jnp.where(pick, NEG_INF, gate)
    sel_f = jnp.where(sel & fully_past, 1.0, 0.0)

    qb = q.astype(BF16)
    row = lax.broadcasted_iota(jnp.int32, (blk, blk), 0)
    col = lax.broadcasted_iota(jnp.int32, (blk, blk), 1)
    dist = row - col

    k_own = k_ref[0, pl.ds(pl.multiple_of(qi * blk, blk), blk), :]
    v_own = v_ref[0, pl.ds(pl.multiple_of(qi * blk, blk), blk), :]
    s = _dot_nt(qb, k_own.astype(BF16)) * scale - slope * dist.astype(F32)
    s = jnp.where(col <= row, s, NEG_INF)
    m0 = jnp.max(s, axis=-1, keepdims=True)
    p = jnp.exp(s - m0)
    l0 = jnp.sum(p, axis=-1, keepdims=True)
    acc0 = _dot(p.astype(BF16), v_own.astype(BF16))

    def body(n, carry):
        m_i, l_i, acc = carry
        off = pl.multiple_of(n * blk, blk)
        kb = k_ref[0, pl.ds(off, blk), :]
        vb = v_ref[0, pl.ds(off, blk), :]
        picked = jnp.sum(jnp.where(lane == n, sel_f, 0.0), axis=-1, keepdims=True) > 0.0
        s = _dot_nt(qb, kb.astype(BF16)) * scale - slope * (dist + (qi - n) * blk).astype(F32)
        s = jnp.where(picked, s, NEG_INF)
        m_new = jnp.maximum(m_i, jnp.max(s, axis=-1, keepdims=True))
        alpha = jnp.exp(m_i - m_new)
        p = jnp.exp(s - m_new)
        l_new = alpha * l_i + jnp.sum(p, axis=-1, keepdims=True)
        acc_new = alpha * acc + _dot(p.astype(BF16), vb.astype(BF16))
        return m_new, l_new, acc_new

    _, l_f, acc_f = lax.fori_loop(0, qi, body, (m0, l0, acc0))
    o_ref[0] = acc_f / l_f


def _moba_prompt(z, *, heads, hd, q_col, k_col, v_col):
    b, t, _ = z.shape
    blk = MOBA_BLOCK
    assert t % blk == 0 and hd == LANES
    nb = t // blk
    assert MOBA_TOPK <= nb <= LANES
    return pl.pallas_call(
        functools.partial(_moba_prompt_kernel, nb=nb, hd=hd, heads=heads),
        grid=(b, heads, nb),
        in_specs=[pl.BlockSpec((1, blk, hd), lambda i, h, j: (i, j, q_col + h)),
                  pl.BlockSpec((1, t, hd), lambda i, h, j: (i, 0, k_col + h)),
                  pl.BlockSpec((1, t, hd), lambda i, h, j: (i, 0, v_col + h))],
        out_specs=pl.BlockSpec((1, blk, hd), lambda i, h, j: (i, j, h)),
        out_shape=jax.ShapeDtypeStruct((b, t, heads * hd), F32),
        scratch_shapes=[pltpu.VMEM((LANES, hd), F32)],
        compiler_params=_cparams("parallel", "parallel", "arbitrary"),
        name="moba_prompt",
    )(z, z, z)


PAGES_PER_STEP = 8


def _block_sum_kernel(pt_ref, *refs, page, blk):
    del pt_ref
    k_refs, o_ref = refs[:-1], refs[-1]
    per_blk = blk // page
    for n in range(len(k_refs) // per_blk):
        acc = jnp.sum(k_refs[n * per_blk][...], axis=0)
        for r in range(1, per_blk):
            acc = acc + jnp.sum(k_refs[n * per_blk + r][...], axis=0)
        o_ref[0, n] = acc


def _paged_block_sums(cache, layer, page_table):
    _, _, page, heads, hd = cache.shape
    b, n_pages = page_table.shape
    pps = PAGES_PER_STEP
    assert MOBA_BLOCK % page == 0 and (pps * page) % MOBA_BLOCK == 0 and n_pages % pps == 0
    blocks_per_step = pps * page // MOBA_BLOCK

    def page_spec(r):
        return pl.BlockSpec((None, None, page, heads, hd),
                            lambda i, s, pt: (layer, pt[i, s * pps + r], 0, 0, 0))

    return pl.pallas_call(
        functools.partial(_block_sum_kernel, page=page, blk=MOBA_BLOCK),
        grid_spec=pltpu.PrefetchScalarGridSpec(
            num_scalar_prefetch=1,
            grid=(b, n_pages // pps),
            in_specs=[page_spec(r) for r in range(pps)],
            out_specs=pl.BlockSpec((1, blocks_per_step, heads, hd), lambda i, s, pt: (i, s, 0, 0)),
        ),
        out_shape=jax.ShapeDtypeStruct((b, n_pages * page // MOBA_BLOCK, heads, hd), F32),
        compiler_params=_cparams("parallel", "arbitrary"),
        name="paged_block_sums",
    )(page_table, *([cache] * pps))


def _decode_select_kernel(q_ref, ks_ref, o_ref, *, nb):
    gate = jnp.sum(ks_ref[0] * q_ref[0][None], axis=-1, keepdims=True) / MOBA_BLOCK
    idx = lax.broadcasted_iota(jnp.int32, gate.shape, 0)
    for r in range(MOBA_TOPK):
        m = jnp.max(gate, axis=0, keepdims=True)
        first = jnp.min(jnp.where(gate == m, idx, nb), axis=0, keepdims=True)
        o_ref[0, r] = jnp.broadcast_to(first[0], o_ref.shape[2:])
        gate = jnp.where(idx == first, NEG_INF, gate)


def _decode_select(q, ksum):
    b, nb, heads, hd = ksum.shape
    assert nb >= MOBA_TOPK
    out = pl.pallas_call(
        functools.partial(_decode_select_kernel, nb=nb),
        grid=(b,),
        in_specs=[pl.BlockSpec((1, heads, hd), lambda i: (i, 0, 0)),
                  pl.BlockSpec((1, nb, heads, hd), lambda i: (i, 0, 0, 0))],
        out_specs=pl.BlockSpec((1, MOBA_TOPK, heads, LANES), lambda i: (i, 0, 0, 0)),
        out_shape=jax.ShapeDtypeStruct((b, MOBA_TOPK, heads, LANES), jnp.int32),
        compiler_params=_cparams("parallel"),
        name="decode_select",
    )(q, ksum)
    return out[..., 0]


def _decode_attn_kernel(pt_ref, sel_ref, q_ref, kn_ref, vn_ref, *refs,
                        page, past_len, per_blk):
    del pt_ref
    k_refs, v_refs = refs[:per_blk], refs[per_blk:2 * per_blk]
    o_ref, m_ref, l_ref, acc_ref = refs[2 * per_blk:]
    i, h, j = pl.program_id(0), pl.program_id(1), pl.program_id(2)
    heads, hd = q_ref.shape[1], q_ref.shape[2]
    q = q_ref[0]
    scale = hd ** -0.5
    head = lax.broadcasted_iota(jnp.int32, (heads, 1), 0)
    slopes = jnp.exp2(-(head + 1).astype(F32) * (8.0 / heads))

    @pl.when((h == 0) & (j == 0))
    def _():
        m_ref[...] = jnp.sum(q * kn_ref[0], axis=-1, keepdims=True) * scale
        l_ref[...] = jnp.ones_like(l_ref)
        acc_ref[...] = vn_ref[0]

    n = sel_ref[(i * MOBA_TOPK + j) * heads + h]
    mine = (head == h)[None]
    r = lax.broadcasted_iota(jnp.int32, (page, 1, 1), 0)
    for part in range(per_blk):
        kpos = n * MOBA_BLOCK + part * page + r
        s = jnp.sum(k_refs[part][...] * q[None], axis=-1, keepdims=True) * scale
        s = s - slopes[None] * (past_len - kpos).astype(F32)
        s = jnp.where(mine, s, NEG_INF)
        m_old = m_ref[...]
        m_new = jnp.maximum(m_old, jnp.max(s, axis=0))
        alpha = jnp.exp(m_old - m_new)
        p = jnp.exp(s - m_new[None])
        l_ref[...] = alpha * l_ref[...] + jnp.sum(p, axis=0)
        acc_ref[...] = alpha * acc_ref[...] + jnp.sum(p * v_refs[part][...], axis=0)
        m_ref[...] = m_new

    @pl.when((h == heads - 1) & (j == MOBA_TOPK - 1))
    def _():
        o_ref[0] = acc_ref[...] / l_ref[...]


def _moba_decode(q, k_new, v_new, cache_k, cache_v, layer, page_table, sel):
    b, heads, hd = q.shape
    page = cache_k.shape[2]
    per_blk = MOBA_BLOCK // page
    past_len = page_table.shape[1] * page
    assert past_len % MOBA_BLOCK == 0

    def page_spec(part):
        def index(i, h, j, pt, sl):
            n = sl[(i * MOBA_TOPK + j) * heads + h]
            return (layer, pt[i, n * per_blk + part], 0, 0, 0)
        return pl.BlockSpec((None, None, page, heads, hd), index)

    tok_spec = pl.BlockSpec((1, heads, hd), lambda i, h, j, pt, sl: (i, 0, 0))
    return pl.pallas_call(
        functools.partial(_decode_attn_kernel, page=page, past_len=past_len, per_blk=per_blk),
        grid_spec=pltpu.PrefetchScalarGridSpec(
            num_scalar_prefetch=2,
            grid=(b, heads, MOBA_TOPK),
            in_specs=[tok_spec, tok_spec, tok_spec]
                     + [page_spec(r) for r in range(per_blk)] * 2,
            out_specs=tok_spec,
            scratch_shapes=[pltpu.VMEM((heads, 1), F32), pltpu.VMEM((heads, 1), F32),
                            pltpu.VMEM((heads, hd), F32)],
        ),
        out_shape=jax.ShapeDtypeStruct((b, heads, hd), F32),
        compiler_params=_cparams("parallel", "arbitrary", "arbitrary"),
        name="moba_decode",
    )(page_table, sel.reshape(-1), q, k_new, v_new,
      *([cache_k] * per_blk), *([cache_v] * per_blk))


def _gla_gate_kernel(x_ref, g_ref, wa_ref, w2_ref, b_ref, o_ref):
    hn = _rmsnorm(x_ref[...], g_ref[...])
    a = _dot(hn.astype(BF16), wa_ref[...])
    z = _dot(a.astype(BF16), w2_ref[...]) + b_ref[...]
    o_ref[...] = jax.nn.log_sigmoid(z) / GLA_GATE_TAU


def _gla_gate(x, g, wa, w2, bias, *, tm=512):
    m, d = x.shape
    n = w2.shape[1]
    tm = _tile(m, tm)
    return pl.pallas_call(
        _gla_gate_kernel,
        grid=(m // tm,),
        in_specs=[pl.BlockSpec((tm, d), lambda i: (i, 0)),
                  pl.BlockSpec((1, d), lambda i: (0, 0)),
                  pl.BlockSpec(wa.shape, lambda i: (0, 0)),
                  pl.BlockSpec(w2.shape, lambda i: (0, 0)),
                  pl.BlockSpec((1, n), lambda i: (0, 0))],
        out_specs=pl.BlockSpec((tm, n), lambda i: (i, 0)),
        out_shape=jax.ShapeDtypeStruct((m, n), F32),
        compiler_params=_cparams("parallel"),
        name="gla_gate",
    )(x, g.reshape(1, d), wa, w2, bias.reshape(1, n))


def _gla_prompt_kernel(q_ref, k_ref, v_ref, r_ref, lg_ref, gg_ref, o_ref, s_ref, *, dk):
    c = GLA_CHUNK

    @pl.when(pl.program_id(2) == 0)
    def _():
        s_ref[...] = jnp.zeros_like(s_ref)

    q = q_ref[0] * dk ** -0.5
    k = k_ref[0]
    v = v_ref[0]
    row = lax.broadcasted_iota(jnp.int32, (c, c), 0)
    col = lax.broadcasted_iota(jnp.int32, (c, c), 1)
    causal = col <= row
    tri = jnp.where(causal, 1.0, 0.0).astype(BF16)
    lg_hi, lg_mid, lg_lo = _split3(lg_ref[0])
    b = _dot(tri, lg_hi) + (_dot(tri, lg_mid) + _dot(tri, lg_lo))
    b_last = b[c - 1:c, :]
    q_dec = (q * jnp.exp(b)).astype(BF16)
    a = _dot_nt(q_dec, (k * jnp.exp(-b)).astype(BF16))
    a = jnp.where(causal, a, 0.0)
    state = s_ref[0, 0]
    o = _dot(a.astype(BF16), v.astype(BF16)) + _dot(q_dec, state.astype(BF16))
    k_rem = (k * jnp.exp(b_last - b)).astype(BF16)
    decay = jnp.exp(jnp.broadcast_to(b_last, (LANES, dk))).T[:, 0:1]
    s_ref[0, 0] = decay * state + _dot_tn(k_rem, v.astype(BF16))
    r = r_ref[0]
    o_ref[0] = _rmsnorm(o, gg_ref[...]) * (r * jax.nn.sigmoid(r))


def _gla_prompt(z, lg, g_gla, *, heads, dk, dv):
    b, t, _ = z.shape
    c = GLA_CHUNK
    assert t % c == 0
    q_col, k_col = 0, heads
    v_col, r_col = 2 * heads * dk // dv, 2 * heads * dk // dv + heads
    return pl.pallas_call(
        functools.partial(_gla_prompt_kernel, dk=dk),
        grid=(b, heads, t // c),
        in_specs=[pl.BlockSpec((1, c, dk), lambda i, h, j: (i, j, q_col + h)),
                  pl.BlockSpec((1, c, dk), lambda i, h, j: (i, j, k_col + h)),
                  pl.BlockSpec((1, c, dv), lambda i, h, j: (i, j, v_col + h)),
                  pl.BlockSpec((1, c, dv), lambda i, h, j: (i, j, r_col + h)),
                  pl.BlockSpec((1, c, dk), lambda i, h, j: (i, j, h)),
                  pl.BlockSpec((1, dv), lambda i, h, j: (0, 0))],
        out_specs=[pl.BlockSpec((1, c, dv), lambda i, h, j: (i, j, h)),
                   pl.BlockSpec((1, 1, dk, dv), lambda i, h, j: (i, h, 0, 0))],
        out_shape=[jax.ShapeDtypeStruct((b, t, heads * dv), F32),
                   jax.ShapeDtypeStruct((b, heads, dk, dv), F32)],
        compiler_params=_cparams("parallel", "parallel", "arbitrary"),
        name="gla_prompt",
    )(z, z, z, z, lg, g_gla.reshape(1, dv))


def _gla_decode_kernel(q_ref, k_ref, lg_ref, v_ref, r_ref, gg_ref, s0_ref, o_ref, s_ref, *, dk):
    q = q_ref[0, 0] * dk ** -0.5
    k = k_ref[0, 0]
    b = lg_ref[0, 0]
    v = v_ref[0, 0]
    state = s0_ref[0, 0]
    q_dec = q * jnp.exp(b)
    a = jnp.sum(q_dec * (k * jnp.exp(-b)), axis=0, keepdims=True)
    o = a * v + jnp.sum(q_dec * state, axis=0, keepdims=True)
    s_ref[0, 0] = jnp.exp(b) * state + (k * jnp.exp(b - b)) * v
    r = r_ref[0, 0]
    o_ref[0, 0] = _rmsnorm(o, gg_ref[...]) * (r * jax.nn.sigmoid(r))


def _gla_decode(q, k, lg, v, r, g_gla, s0):
    b, heads, dk, dv = s0.shape
    col = pl.BlockSpec((1, 1, dk, 1), lambda i, h: (i, h, 0, 0))
    rowspec = pl.BlockSpec((1, 1, 1, dv), lambda i, h: (i, h, 0, 0))
    st = pl.BlockSpec((1, 1, dk, dv), lambda i, h: (i, h, 0, 0))
    return pl.pallas_call(
        functools.partial(_gla_decode_kernel, dk=dk),
        grid=(b, heads),
        in_specs=[col, col, col, rowspec, rowspec,
                  pl.BlockSpec((1, dv), lambda i, h: (0, 0)), st],
        out_specs=[rowspec, st],
        out_shape=[jax.ShapeDtypeStruct((b, heads, 1, dv), F32),
                   jax.ShapeDtypeStruct((b, heads, dk, dv), s0.dtype)],
        compiler_params=_cparams("parallel", "parallel"),
        name="gla_decode",
    )(q, k, lg, v, r, g_gla.reshape(1, dv), s0)


def _trunk(x, p, start, pool_state, gla_state, cache_k, cache_v, page_table, wts):
    b, t, d = x.shape
    m = b * t
    depth = wts["g_mix"].shape[0]
    pool_w = wts["pool_scale"].shape[1]
    pool_buf = max(POOL_WINDOWS) - 1
    heads_m, hd = cache_k.shape[3], cache_k.shape[4]
    moba_w = heads_m * hd
    _, _, heads_g, dk, dv = gla_state.shape
    dk_tot, dv_tot = heads_g * dk, heads_g * dv
    decode = page_table is not None

    h = x.reshape(m, d)
    pools, ks, vs, glas = [], [], [], []
    for i in range(depth):
        j = i // 2
        if i % 2 == 0:
            z = _norm_matmul(h, wts["g_mix"][i], wts["w_in_even"][j]).reshape(b, t, -1)
            u = z[..., :pool_w]
            k = z[..., pool_w + moba_w:pool_w + 2 * moba_w].reshape(b, t, heads_m, hd)
            v = z[..., pool_w + 2 * moba_w:].reshape(b, t, heads_m, hd)
            if decode:
                buf = pool_state[j]
                pools.append(jnp.concatenate([buf, u], axis=1)[:, -pool_buf:])
            else:
                buf = jnp.zeros((b, pool_buf, pool_w), x.dtype)
                pools.append(u[:, -pool_buf:])
            y_a = _pool_mixer(z, buf, start, wts["w_pool"][j], wts["pool_scale"][j], width=pool_w)
            if decode:
                q = z[..., pool_w:pool_w + moba_w].reshape(b, heads_m, hd)
                ksum = _paged_block_sums(cache_k, j, page_table)
                sel = _decode_select(q, ksum)
                y_b = _moba_decode(q, k.reshape(b, heads_m, hd), v.reshape(b, heads_m, hd),
                                   cache_k, cache_v, j, page_table, sel)
            else:
                y_b = _moba_prompt(z, heads=heads_m, hd=hd, q_col=pool_w // hd,
                                   k_col=(pool_w + moba_w) // hd, v_col=(pool_w + 2 * moba_w) // hd)
            w_out = wts["w_out_even"][j]
            h = _proj_res([y_a.reshape(m, pool_w), y_b.reshape(m, moba_w)],
                          [w_out[:pool_w], w_out[pool_w:]], h)
            ks.append(k)
            vs.append(v)
        else:
            n_main = 2 * dk_tot + 2 * dv_tot
            w_in = wts["w_in_odd"][j]
            z = _norm_matmul(h, wts["g_mix"][i], w_in[:, :n_main]).reshape(b, t, n_main)
            lg = _gla_gate(h, wts["g_mix"][i], w_in[:, n_main:], wts["w_gk2"][j], wts["b_gk"][j])
            if decode:
                def cols(a):
                    return a.reshape(b, heads_g, dk, 1)
                def rows(a):
                    return a.reshape(b, heads_g, 1, dv)
                z2 = z.reshape(b, n_main)
                og, s_new = _gla_decode(
                    cols(z2[:, :dk_tot]), cols(z2[:, dk_tot:2 * dk_tot]), cols(lg),
                    rows(z2[:, 2 * dk_tot:2 * dk_tot + dv_tot]), rows(z2[:, 2 * dk_tot + dv_tot:]),
                    wts["g_gla"][j], gla_state[j])
            else:
                og, s_new = _gla_prompt(z, lg.reshape(b, t, dk_tot), wts["g_gla"][j],
                                        heads=heads_g, dk=dk, dv=dv)
            h = _proj_res([og.reshape(m, dv_tot)], [wts["w_out_odd"][j]], h)
            glas.append(s_new.astype(gla_state.dtype))
        h = _ffn(h, wts["g_ffn"][i], wts["w_ffn_gate"][i], wts["w_ffn_up"][i], wts["w_ffn_down"][i])
        h = _ple(h, wts["g_ple"][i], wts["w_ple_gate"][i], p[i].reshape(m, -1), wts["w_ple_proj"][i])
    y = _final_norm(h, wts["g_final"]).reshape(b, t, d)
    return y, jnp.stack(pools), jnp.stack(ks), jnp.stack(vs), jnp.stack(glas)


_MATMUL_WEIGHTS = ("w_in_even", "w_pool", "w_out_even", "w_in_odd", "w_gk2", "w_out_odd",
                   "w_ffn_gate", "w_ffn_up", "w_ffn_down", "w_ple_gate", "w_ple_proj")


def kernel(x_prompt, x_sample, state_pool, cache_k, cache_v, state_gla, page_table, p_prompt, p_sample,
           g_mix, g_ffn, g_ple, g_final, w_in_even, w_pool, pool_scale, w_out_even,
           w_in_odd, w_gk2, b_gk, g_gla, w_out_odd, w_ffn_gate, w_ffn_up, w_ffn_down,
           w_ple_gate, w_ple_proj):
    wts = dict(g_mix=g_mix, g_ffn=g_ffn, g_ple=g_ple, g_final=g_final, w_in_even=w_in_even,
               w_pool=w_pool, pool_scale=pool_scale, w_out_even=w_out_even, w_in_odd=w_in_odd,
               w_gk2=w_gk2, b_gk=b_gk, g_gla=g_gla, w_out_odd=w_out_odd, w_ffn_gate=w_ffn_gate,
               w_ffn_up=w_ffn_up, w_ffn_down=w_ffn_down, w_ple_gate=w_ple_gate, w_ple_proj=w_ple_proj)
    for name in _MATMUL_WEIGHTS:
        wts[name] = wts[name].astype(BF16)
    past_len = page_table.shape[1] * cache_k.shape[2]
    gla0 = jnp.zeros((state_gla.shape[0], x_prompt.shape[0]) + state_gla.shape[2:], state_gla.dtype)
    y_p, pool_p, k_p, v_p, gla_p = _trunk(x_prompt, p_prompt, 0, None, gla0, cache_k, cache_v, None, wts)
    y_s, pool_s, k_s, v_s, gla_s = _trunk(x_sample, p_sample, past_len, state_pool, state_gla,
                                          cache_k, cache_v, page_table, wts)
    return (y_p, y_s, pool_p, pool_s, k_p, k_s, v_p, v_s, gla_p, gla_s)
```

```python
import functools

import jax
import jax.numpy as jnp
from jax import lax
from jax.experimental import pallas as pl
from jax.experimental.pallas import tpu as pltpu

F32 = jnp.float32
BF16 = jnp.bfloat16

RMS_EPS = 1e-6
POOL_WINDOWS = (2, 4, 8, 16)
POOL_HALO = 16
MOBA_BLOCK = 256
MOBA_TOPK = 3
MOBA_VARIANT_BLOCKS = 4
GLA_GATE_TAU = 16.0
GLA_CHUNK = 64
LANES = 128
SUBLANES = 8
VMEM_LIMIT = 52 * 1024 * 1024
NEG_INF = float("-inf")
MASKED = -1e30


def _cparams(*sem):
    return pltpu.CompilerParams(dimension_semantics=sem, vmem_limit_bytes=VMEM_LIMIT)


def _rmsnorm(x, g):
    ms = jnp.mean(x * x, axis=-1, keepdims=True)
    return x * lax.rsqrt(ms + RMS_EPS) * g


def _dot(a, b):
    return jnp.dot(a, b, preferred_element_type=F32)


def _wdot(a, w):
    if w.dtype == F32:
        return jnp.dot(a.astype(F32), w, precision=lax.Precision.HIGHEST, preferred_element_type=F32)
    return _dot(a.astype(BF16), w)


def _dot_nt(a, b):
    return lax.dot_general(a, b, (((1,), (1,)), ((), ())), preferred_element_type=F32)


def _dot_tn(a, b):
    return lax.dot_general(a, b, (((0,), (0,)), ((), ())), preferred_element_type=F32)


def _split2(a):
    hi = a.astype(BF16)
    lo = (a - hi.astype(F32)).astype(BF16)
    return hi, lo


def _split3(a):
    hi = a.astype(BF16)
    r = a - hi.astype(F32)
    mid = r.astype(BF16)
    lo = (r - mid.astype(F32)).astype(BF16)
    return hi, mid, lo


def _dot_nt_3pass(a, b):
    ah, al = _split2(a)
    bh, bl = _split2(b)
    return _dot_nt(ah, bh) + (_dot_nt(ah, bl) + _dot_nt(al, bh))


def _tile(m, t):
    t = min(m, t)
    assert m % t == 0, (m, t)
    return t


def _act_dtype(rows):
    return BF16 if rows % 16 == 0 else F32


def _layer_spec(tail, layer, index):
    return pl.BlockSpec((None,) + tuple(tail), lambda *a: (layer,) + tuple(index(*a)))


def _gain(g):
    return g.reshape(g.shape[0], 1, g.shape[1])


def _norm_matmul_kernel(x_ref, g_ref, w_ref, *refs, steps_per_out):
    o_refs, hn_ref = refs[:-1], refs[-1]
    j = pl.program_id(1)

    @pl.when(j == 0)
    def _():
        hn_ref[...] = _rmsnorm(x_ref[...], g_ref[...]).astype(hn_ref.dtype)

    res = _wdot(hn_ref[...], w_ref[...])
    if len(o_refs) == 1:
        o_refs[0][...] = res
    else:
        for idx, o_ref in enumerate(o_refs):
            @pl.when(j // steps_per_out == idx)
            def _(o_ref=o_ref):
                o_ref[...] = res


def _norm_matmul(x, g, w, layer, wlayer, n, *, n_out=1, tm=1024, tn=512):
    m, d = x.shape
    group = n // n_out
    tm, tn = _tile(m, tm), _tile(group, tn)
    spo = group // tn

    def out_index(idx):
        return lambda i, j: (i, jnp.clip(j - idx * spo, 0, spo - 1))

    outs = pl.pallas_call(
        functools.partial(_norm_matmul_kernel, steps_per_out=spo),
        grid=(m // tm, n // tn),
        in_specs=[pl.BlockSpec((tm, d), lambda i, j: (i, 0)),
                  _layer_spec((1, d), layer, lambda i, j: (0, 0)),
                  _layer_spec((d, tn), wlayer, lambda i, j: (0, j))],
        out_specs=[pl.BlockSpec((tm, tn), out_index(idx)) for idx in range(n_out)],
        out_shape=[jax.ShapeDtypeStruct((m, group), F32)] * n_out,
        scratch_shapes=[pltpu.VMEM((tm, d), _act_dtype(tm))],
        compiler_params=_cparams("parallel", "arbitrary"),
        name="norm_matmul",
    )(x, _gain(g), w)
    return outs[0] if n_out == 1 else outs


def _proj_res_kernel(*refs, n_in):
    a_refs, w_refs = refs[:n_in], refs[n_in:2 * n_in]
    res_ref, o_ref = refs[2 * n_in], refs[2 * n_in + 1]
    acc = res_ref[...]
    for a_ref, w_ref in zip(a_refs, w_refs):
        acc = acc + _wdot(a_ref[...], w_ref[...])
    o_ref[...] = acc


def _proj_res(a_list, w, layer, res, *, tm=1024, tn=512):
    m, n = res.shape
    tm, tn = _tile(m, tm), _tile(n, tn)
    n_in = len(a_list)
    kw = a_list[0].shape[1]
    assert all(a.shape[1] == kw for a in a_list) and kw * n_in == w.shape[1]
    in_specs = ([pl.BlockSpec((tm, kw), lambda i, j: (i, 0)) for _ in a_list]
                + [_layer_spec((kw, tn), layer, lambda i, j, r=r: (r, j)) for r in range(n_in)]
                + [pl.BlockSpec((tm, tn), lambda i, j: (i, j))])
    return pl.pallas_call(
        functools.partial(_proj_res_kernel, n_in=n_in),
        grid=(m // tm, n // tn),
        in_specs=in_specs,
        out_specs=pl.BlockSpec((tm, tn), lambda i, j: (i, j)),
        out_shape=jax.ShapeDtypeStruct((m, n), F32),
        compiler_params=_cparams("parallel", "arbitrary"),
        name="proj_res",
    )(*a_list, *([w] * n_in), res)


def _ffn_kernel(x_ref, g_ref, wg_ref, wu_ref, wd_ref, o_ref, hn_ref):
    @pl.when(pl.program_id(1) == 0)
    def _():
        x = x_ref[...]
        hn_ref[...] = _rmsnorm(x, g_ref[...]).astype(hn_ref.dtype)
        o_ref[...] = x

    hn = hn_ref[...]
    a = _wdot(hn, wg_ref[...])
    b = _wdot(hn, wu_ref[...])
    h = (a * jax.nn.sigmoid(a)) * b
    o_ref[...] += _wdot(h, wd_ref[...])


def _ffn(x, g, wg, wu, wd, layer, *, tm=512, tf=512):
    m, d = x.shape
    f = wg.shape[2]
    tm, tf = _tile(m, tm), _tile(f, tf)
    return pl.pallas_call(
        _ffn_kernel,
        grid=(m // tm, f // tf),
        in_specs=[pl.BlockSpec((tm, d), lambda i, j: (i, 0)),
                  _layer_spec((1, d), layer, lambda i, j: (0, 0)),
                  _layer_spec((d, tf), layer, lambda i, j: (0, j)),
                  _layer_spec((d, tf), layer, lambda i, j: (0, j)),
                  _layer_spec((tf, d), layer, lambda i, j: (j, 0))],
        out_specs=pl.BlockSpec((tm, d), lambda i, j: (i, 0)),
        out_shape=jax.ShapeDtypeStruct((m, d), F32),
        scratch_shapes=[pltpu.VMEM((tm, d), _act_dtype(tm))],
        compiler_params=_cparams("parallel", "arbitrary"),
        name="ffn",
    )(x, _gain(g), wg, wu, wd)


def _ple_kernel(x_ref, xj_ref, g_ref, wg_ref, p_ref, wp_ref, o_ref, hn_ref):
    @pl.when(pl.program_id(1) == 0)
    def _():
        hn_ref[...] = _rmsnorm(x_ref[...], g_ref[...]).astype(hn_ref.dtype)

    gate = _wdot(hn_ref[...], wg_ref[...])
    proj = _wdot(p_ref[...], wp_ref[...])
    o_ref[...] = xj_ref[...] + jax.nn.sigmoid(gate) * proj


def _ple(x, g, wg, p, wp, layer, *, tm=1024, tn=512):
    m, d = x.shape
    tm, tn = _tile(m, tm), _tile(d, tn)
    pd = p.shape[2]
    return pl.pallas_call(
        _ple_kernel,
        grid=(m // tm, d // tn),
        in_specs=[pl.BlockSpec((tm, d), lambda i, j: (i, 0)),
                  pl.BlockSpec((tm, tn), lambda i, j: (i, j)),
                  _layer_spec((1, d), layer, lambda i, j: (0, 0)),
                  _layer_spec((d, tn), layer, lambda i, j: (0, j)),
                  _layer_spec((tm, pd), layer, lambda i, j: (i, 0)),
                  _layer_spec((pd, tn), layer, lambda i, j: (0, j))],
        out_specs=pl.BlockSpec((tm, tn), lambda i, j: (i, j)),
        out_shape=jax.ShapeDtypeStruct((m, d), F32),
        scratch_shapes=[pltpu.VMEM((tm, d), _act_dtype(tm))],
        compiler_params=_cparams("parallel", "arbitrary"),
        name="ple",
    )(x, x, _gain(g), wg, p, wp)


def _final_norm_kernel(x_ref, g_ref, o_ref):
    o_ref[...] = _rmsnorm(x_ref[...], g_ref[...])


def _final_norm(x, g, *, tm=512):
    m, d = x.shape
    tm = _tile(m, tm)
    return pl.pallas_call(
        _final_norm_kernel,
        grid=(m // tm,),
        in_specs=[pl.BlockSpec((tm, d), lambda i: (i, 0)),
                  pl.BlockSpec((1, d), lambda i: (0, 0))],
        out_specs=pl.BlockSpec((tm, d), lambda i: (i, 0)),
        out_shape=jax.ShapeDtypeStruct((m, d), F32),
        compiler_params=_cparams("parallel"),
        name="final_norm",
    )(x, g.reshape(1, d))


def _pool_kernel(u_ref, halo_ref, w_ref, s_ref, o_ref, ext_ref, *, tt, start, gw):
    t = pl.program_id(1)

    @pl.when(t == 0)
    def _():
        ext_ref[0:POOL_HALO, :] = halo_ref[0]

    @pl.when(t > 0)
    def _():
        ext_ref[0:POOL_HALO, :] = ext_ref[tt:tt + POOL_HALO, :]

    ext_ref[POOL_HALO:POOL_HALO + tt, :] = u_ref[0]
    pos = start + t * tt + lax.broadcasted_iota(jnp.int32, (tt, 1), 0)
    for g, w in enumerate(POOL_WINDOWS):
        c0, c1 = g * gw, (g + 1) * gw
        cur = ext_ref[POOL_HALO:POOL_HALO + tt, c0:c1]
        acc = cur
        for k in range(1, w):
            acc = acc + ext_ref[POOL_HALO - k:POOL_HALO - k + tt, c0:c1]
        cnt = jnp.minimum(pos + 1, w).astype(F32)
        d = acc / cnt - cur
        y = _wdot(d, w_ref[g])
        o_ref[0, :, c0:c1] = (y * s_ref[:, c0:c1]).astype(o_ref.dtype)


def _pool_mixer(u, buf, start, w_pool, scale, layer, *, tt=512):
    b, t, width = u.shape
    tt = _tile(t, tt)
    gw = width // len(POOL_WINDOWS)
    halo = jnp.pad(buf, ((0, 0), (POOL_HALO - buf.shape[1], 0), (0, 0)))
    return pl.pallas_call(
        functools.partial(_pool_kernel, tt=tt, start=start, gw=gw),
        grid=(b, t // tt),
        in_specs=[pl.BlockSpec((1, tt, width), lambda i, j: (i, j, 0)),
                  pl.BlockSpec((1, POOL_HALO, width), lambda i, j: (i, 0, 0)),
                  _layer_spec(w_pool.shape[1:], layer, lambda i, j: (0, 0, 0)),
                  _layer_spec((1, width), layer, lambda i, j: (0, 0))],
        out_specs=pl.BlockSpec((1, tt, width), lambda i, j: (i, j, 0)),
        out_shape=jax.ShapeDtypeStruct((b, t, width), _act_dtype(tt)),
        scratch_shapes=[pltpu.VMEM((POOL_HALO + max(tt, POOL_HALO), width), F32)],
        compiler_params=_cparams("parallel", "arbitrary"),
        name="pool_mixer",
    )(u, halo, w_pool, _gain(scale))


def _moba_prompt_kernel(q_ref, k_ref, v_ref, o_ref, kmean_ref, kaug_ref, vaug_ref, aug_ref,
                        *, nb, nbp, hd, heads):
    h = pl.program_id(1)
    qi = pl.program_id(2)
    blk = MOBA_BLOCK
    slope = jnp.exp2(-jnp.full((1, 1), h + 1, jnp.int32).astype(F32) * (8.0 / heads))
    lane = lax.broadcasted_iota(jnp.int32, (blk, LANES), 1)

    @pl.when(qi == 0)
    def _():
        kmean_ref[...] = jnp.zeros_like(kmean_ref)
        offs = lax.broadcasted_iota(jnp.int32, (blk, LANES), 0).astype(F32)
        ones_col = jnp.where(lane == 0, 1.0, 0.0).astype(BF16)
        for n in range(nb):
            rows = slice(n * blk, (n + 1) * blk)
            kb = k_ref[0, rows, :]
            kmean_ref[n:n + 1, :] = jnp.sum(kb, axis=0, keepdims=True) / blk
            ext = jnp.where(lane == n, 1.0, 0.0)
            ext = jnp.where((lane >= nbp) & (lane < nbp + 3), float(n * blk), ext)
            ext = jnp.where((lane >= nbp + 3) & (lane < nbp + 6), offs, ext)
            kaug_ref[rows, 0:hd] = kb.astype(BF16)
            kaug_ref[rows, hd:] = ext.astype(BF16)
            vaug_ref[rows, 0:hd] = v_ref[0, rows, :].astype(BF16)
            vaug_ref[rows, hd:] = ones_col

    q = q_ref[0]

    bid = lax.broadcasted_iota(jnp.int32, (nbp, blk), 0)
    bid_f = bid.astype(F32)
    fully_past = bid < qi
    gate = jnp.where(fully_past, _dot_nt_3pass(kmean_ref[...], q), NEG_INF)
    sel = jnp.zeros((nbp, blk), jnp.bool_)
    for _ in range(MOBA_TOPK):
        m = jnp.max(gate, axis=0, keepdims=True)
        first = jnp.min(jnp.where(gate == m, bid_f, float(nbp)), axis=0, keepdims=True)
        pick = bid_f == first
        sel = sel | pick
        gate = jnp.where(pick, NEG_INF, gate)

    s_hi, s_mid, s_lo = (p.astype(F32) for p in _split3(slope))
    part = lax.broadcasted_iota(jnp.int32, (SUBLANES, blk), 0)
    slope_rows = jnp.where(part % 3 == 0, s_hi, jnp.where(part % 3 == 1, s_mid, s_lo))
    slope_rows = jnp.where(part < 6, slope_rows, 0.0)
    aug_ref[...] = jnp.zeros_like(aug_ref)
    aug_ref[0:nbp, :] = jnp.where(sel & fully_past, 0.0, MASKED)
    aug_ref[nbp:nbp + SUBLANES, :] = slope_rows
    q_ext = aug_ref[...].T
    qs = (q * hd ** -0.5).astype(BF16)
    q_aug = jnp.concatenate([qs, q_ext.astype(BF16)], axis=1)
    q_own = jnp.concatenate([qs, jnp.where(lane < nbp, 0.0, q_ext).astype(BF16)], axis=1)

    own = pl.ds(pl.multiple_of(qi * blk, blk), blk)
    k_own = kaug_ref[own, :]
    v_own = vaug_ref[own, :]
    row = lax.broadcasted_iota(jnp.int32, (blk, blk), 0)
    col = lax.broadcasted_iota(jnp.int32, (blk, blk), 1)
    s_own = jnp.where(col <= row, _dot_nt(q_own, k_own), MASKED)
    m_own = jnp.max(s_own, axis=-1, keepdims=True)

    def finish(n_past):
        if n_past:
            s_past = _dot_nt(q_aug, kaug_ref[0:n_past * blk, :])
            m = jnp.maximum(m_own, jnp.max(s_past, axis=-1, keepdims=True))
        else:
            m = m_own
        acc = _dot(jnp.exp(s_own - m).astype(BF16), v_own)
        if n_past:
            acc = acc + _dot(jnp.exp(s_past - m).astype(BF16), vaug_ref[0:n_past * blk, :])
        o_ref[0] = (acc[:, 0:hd] / acc[:, hd:hd + 1]).astype(o_ref.dtype)

    @pl.when(qi == 0)
    def _():
        finish(0)

    lo = 0
    while lo < nb - 1:
        hi = min(lo + MOBA_VARIANT_BLOCKS, nb - 1)

        @pl.when((qi > lo) & (qi <= hi))
        def _(hi=hi):
            finish(hi)

        lo = hi


def _moba_prompt(q, k, v):
    b, t, width = q.shape
    blk, hd = MOBA_BLOCK, LANES
    heads = width // hd
    assert t % blk == 0 and width % hd == 0
    nb = t // blk
    nbp = -(-nb // SUBLANES) * SUBLANES
    assert nb >= MOBA_TOPK and nbp + SUBLANES <= LANES
    return pl.pallas_call(
        functools.partial(_moba_prompt_kernel, nb=nb, nbp=nbp, hd=hd, heads=heads),
        grid=(b, heads, nb),
        in_specs=[pl.BlockSpec((1, blk, hd), lambda i, h, j: (i, j, h)),
                  pl.BlockSpec((1, t, hd), lambda i, h, j: (i, 0, h)),
                  pl.BlockSpec((1, t, hd), lambda i, h, j: (i, 0, h))],
        out_specs=pl.BlockSpec((1, blk, hd), lambda i, h, j: (i, j, h)),
        out_shape=jax.ShapeDtypeStruct((b, t, width), BF16),
        scratch_shapes=[pltpu.VMEM((nbp, hd), F32),
                        pltpu.VMEM((t, 2 * hd), BF16),
                        pltpu.VMEM((t, 2 * hd), BF16),
                        pltpu.VMEM((LANES, blk), F32)],
        compiler_params=_cparams("parallel", "parallel", "arbitrary"),
        name="moba_prompt",
    )(q, k, v)


PAGES_PER_STEP = 16


def _block_sum_kernel(pt_ref, *refs, page, blk):
    del pt_ref
    k_refs, o_ref = refs[:-1], refs[-1]
    per_blk = blk // page
    for n in range(len(k_refs) // per_blk):
        acc = jnp.sum(k_refs[n * per_blk][...], axis=0)
        for r in range(1, per_blk):
            acc = acc + jnp.sum(k_refs[n * per_blk + r][...], axis=0)
        o_ref[0, n] = acc


def _paged_block_sums(cache, layer, page_table):
    _, _, page, heads, hd = cache.shape
    b, n_pages = page_table.shape
    pps = PAGES_PER_STEP
    assert MOBA_BLOCK % page == 0 and (pps * page) % MOBA_BLOCK == 0 and n_pages % pps == 0
    blocks_per_step = pps * page // MOBA_BLOCK

    def page_spec(r):
        return pl.BlockSpec((None, None, page, heads, hd),
                            lambda i, s, pt: (layer, pt[i, s * pps + r], 0, 0, 0))

    return pl.pallas_call(
        functools.partial(_block_sum_kernel, page=page, blk=MOBA_BLOCK),
        grid_spec=pltpu.PrefetchScalarGridSpec(
            num_scalar_prefetch=1,
            grid=(b, n_pages // pps),
            in_specs=[page_spec(r) for r in range(pps)],
            out_specs=pl.BlockSpec((1, blocks_per_step, heads, hd), lambda i, s, pt: (i, s, 0, 0)),
        ),
        out_shape=jax.ShapeDtypeStruct((b, n_pages * page // MOBA_BLOCK, heads, hd), F32),
        compiler_params=_cparams("parallel", "arbitrary"),
        name="paged_block_sums",
    )(page_table, *([cache] * pps))


def _decode_select_kernel(q_ref, ks_ref, o_ref, *, nb):
    gate = jnp.sum(ks_ref[0] * q_ref[0][None], axis=-1, keepdims=True) / MOBA_BLOCK
    idx = lax.broadcasted_iota(jnp.int32, gate.shape, 0)
    for r in range(MOBA_TOPK):
        m = jnp.max(gate, axis=0, keepdims=True)
        first = jnp.min(jnp.where(gate == m, idx, nb), axis=0, keepdims=True)
        o_ref[0, r] = jnp.broadcast_to(first[0], o_ref.shape[2:])
        gate = jnp.where(idx == first, NEG_INF, gate)


def _decode_select(q, ksum):
    b, nb, heads, hd = ksum.shape
    assert nb >= MOBA_TOPK
    out = pl.pallas_call(
        functools.partial(_decode_select_kernel, nb=nb),
        grid=(b,),
        in_specs=[pl.BlockSpec((1, heads, hd), lambda i: (i, 0, 0)),
                  pl.BlockSpec((1, nb, heads, hd), lambda i: (i, 0, 0, 0))],
        out_specs=pl.BlockSpec((1, MOBA_TOPK, heads, LANES), lambda i: (i, 0, 0, 0)),
        out_shape=jax.ShapeDtypeStruct((b, MOBA_TOPK, heads, LANES), jnp.int32),
        compiler_params=_cparams("parallel"),
        name="decode_select",
    )(q, ksum)
    return out[..., 0]


def _decode_attn_kernel(pt_ref, sel_ref, q_ref, kn_ref, vn_ref, *refs,
                        page, past_len, per_blk):
    del pt_ref
    k_refs, v_refs = refs[:per_blk], refs[per_blk:2 * per_blk]
    o_ref, m_ref, l_ref, acc_ref = refs[2 * per_blk:]
    i, h, j = pl.program_id(0), pl.program_id(1), pl.program_id(2)
    heads, hd = q_ref.shape[1], q_ref.shape[2]
    q = q_ref[0]
    scale = hd ** -0.5
    head = lax.broadcasted_iota(jnp.int32, (heads, 1), 0)
    slopes = jnp.exp2(-(head + 1).astype(F32) * (8.0 / heads))

    @pl.when((h == 0) & (j == 0))
    def _():
        m_ref[...] = jnp.sum(q * kn_ref[0], axis=-1, keepdims=True) * scale
        l_ref[...] = jnp.ones_like(l_ref)
        acc_ref[...] = vn_ref[0]

    n = sel_ref[(i * MOBA_TOPK + j) * heads + h]
    mine = (head == h)[None]
    r = lax.broadcasted_iota(jnp.int32, (page, 1, 1), 0)
    for part in range(per_blk):
        kpos = n * MOBA_BLOCK + part * page + r
        s = jnp.sum(k_refs[part][...] * q[None], axis=-1, keepdims=True) * scale
        s = s - slopes[None] * (past_len - kpos).astype(F32)
        s = jnp.where(mine, s, NEG_INF)
        m_old = m_ref[...]
        m_new = jnp.maximum(m_old, jnp.max(s, axis=0))
        alpha = jnp.exp(m_old - m_new)
        p = jnp.exp(s - m_new[None])
        l_ref[...] = alpha * l_ref[...] + jnp.sum(p, axis=0)
        acc_ref[...] = alpha * acc_ref[...] + jnp.sum(p * v_refs[part][...], axis=0)
        m_ref[...] = m_new

    @pl.when((h == heads - 1) & (j == MOBA_TOPK - 1))
    def _():
        o_ref[0] = acc_ref[...] / l_ref[...]


def _moba_decode(q, k_new, v_new, cache_k, cache_v, layer, page_table, sel):
    b, heads, hd = q.shape
    page = cache_k.shape[2]
    per_blk = MOBA_BLOCK // page
    past_len = page_table.shape[1] * page
    assert past_len % MOBA_BLOCK == 0

    def page_spec(part):
        def index(i, h, j, pt, sl):
            n = sl[(i * MOBA_TOPK + j) * heads + h]
            return (layer, pt[i, n * per_blk + part], 0, 0, 0)
        return pl.BlockSpec((None, None, page, heads, hd), index)

    tok_spec = pl.BlockSpec((1, heads, hd), lambda i, h, j, pt, sl: (i, 0, 0))
    return pl.pallas_call(
        functools.partial(_decode_attn_kernel, page=page, past_len=past_len, per_blk=per_blk),
        grid_spec=pltpu.PrefetchScalarGridSpec(
            num_scalar_prefetch=2,
            grid=(b, heads, MOBA_TOPK),
            in_specs=[tok_spec, tok_spec, tok_spec]
                     + [page_spec(r) for r in range(per_blk)] * 2,
            out_specs=tok_spec,
            scratch_shapes=[pltpu.VMEM((heads, 1), F32), pltpu.VMEM((heads, 1), F32),
                            pltpu.VMEM((heads, hd), F32)],
        ),
        out_shape=jax.ShapeDtypeStruct((b, heads, hd), F32),
        compiler_params=_cparams("parallel", "arbitrary", "arbitrary"),
        name="moba_decode",
    )(page_table, sel.reshape(-1), q, k_new, v_new,
      *([cache_k] * per_blk), *([cache_v] * per_blk))


def _gla_gate_kernel(x_ref, g_ref, wa_ref, w2_ref, b_ref, o_ref):
    hn = _rmsnorm(x_ref[...], g_ref[...])
    a = _wdot(hn, wa_ref[...])
    z = _wdot(a, w2_ref[...]) + b_ref[...]
    o_ref[...] = jax.nn.log_sigmoid(z) / GLA_GATE_TAU


def _gla_gate(x, g, wa, w2, bias, layer, wlayer, *, tm=512):
    m, d = x.shape
    rank, n = w2.shape[1:]
    tm = _tile(m, tm)
    return pl.pallas_call(
        _gla_gate_kernel,
        grid=(m // tm,),
        in_specs=[pl.BlockSpec((tm, d), lambda i: (i, 0)),
                  _layer_spec((1, d), layer, lambda i: (0, 0)),
                  _layer_spec((d, rank), wlayer, lambda i: (0, 0)),
                  _layer_spec((rank, n), wlayer, lambda i: (0, 0)),
                  _layer_spec((1, n), wlayer, lambda i: (0, 0))],
        out_specs=pl.BlockSpec((tm, n), lambda i: (i, 0)),
        out_shape=jax.ShapeDtypeStruct((m, n), F32),
        compiler_params=_cparams("parallel"),
        name="gla_gate",
    )(x, _gain(g), wa, w2, _gain(bias))


def _gla_prompt_kernel(z_ref, lg_ref, gg_ref, o_ref, s_ref, *, heads, dk, dv):
    c = GLA_CHUNK

    @pl.when(pl.program_id(0) == 0)
    def _():
        s_ref[...] = jnp.zeros_like(s_ref)

    row = lax.broadcasted_iota(jnp.int32, (c, c), 0)
    col = lax.broadcasted_iota(jnp.int32, (c, c), 1)
    causal = col <= row
    tri = jnp.where(causal, 1.0, 0.0).astype(BF16)
    k0, v0, r0 = heads * dk, 2 * heads * dk, 2 * heads * dk + heads * dv
    for bi in range(z_ref.shape[0]):
        for h in range(heads):
            q = z_ref[bi, :, h * dk:(h + 1) * dk] * dk ** -0.5
            k = z_ref[bi, :, k0 + h * dk:k0 + (h + 1) * dk]
            v = z_ref[bi, :, v0 + h * dv:v0 + (h + 1) * dv].astype(BF16)
            lg_hi, lg_mid, lg_lo = _split3(lg_ref[bi, :, h * dk:(h + 1) * dk])
            b = _dot(tri, lg_hi) + (_dot(tri, lg_mid) + _dot(tri, lg_lo))
            b_last = b[c - 1:c, :]
            q_dec = (q * jnp.exp(b)).astype(BF16)
            a = _dot_nt(q_dec, (k * jnp.exp(-b)).astype(BF16))
            a = jnp.where(causal, a, 0.0)
            state = s_ref[bi, h]
            o = _dot(a.astype(BF16), v) + _dot(q_dec, state.astype(BF16))
            k_rem = (k * jnp.exp(b_last - b)).astype(BF16)
            decay = jnp.exp(jnp.broadcast_to(b_last, (LANES, dk))).T[:, 0:1]
            s_ref[bi, h] = decay * state + _dot_tn(k_rem, v)
            r = z_ref[bi, :, r0 + h * dv:r0 + (h + 1) * dv]
            o_ref[bi, :, h * dv:(h + 1) * dv] = (
                _rmsnorm(o, gg_ref[...]) * (r * jax.nn.sigmoid(r))).astype(o_ref.dtype)


def _gla_prompt(z, lg, g_gla, layer, *, heads, dk, dv):
    b, t, zw = z.shape
    c = GLA_CHUNK
    assert t % c == 0 and zw == 2 * heads * (dk + dv)
    return pl.pallas_call(
        functools.partial(_gla_prompt_kernel, heads=heads, dk=dk, dv=dv),
        grid=(t // c,),
        in_specs=[pl.BlockSpec((b, c, zw), lambda j: (0, j, 0)),
                  pl.BlockSpec((b, c, heads * dk), lambda j: (0, j, 0)),
                  _layer_spec((1, dv), layer, lambda j: (0, 0))],
        out_specs=[pl.BlockSpec((b, c, heads * dv), lambda j: (0, j, 0)),
                   pl.BlockSpec((b, heads, dk, dv), lambda j: (0, 0, 0, 0))],
        out_shape=[jax.ShapeDtypeStruct((b, t, heads * dv), BF16),
                   jax.ShapeDtypeStruct((b, heads, dk, dv), F32)],
        compiler_params=_cparams("arbitrary"),
        name="gla_prompt",
    )(z, lg, _gain(g_gla))


def _gla_decode_kernel(q_ref, k_ref, lg_ref, v_ref, r_ref, gg_ref, s0_ref, o_ref, s_ref, *, dk):
    q = q_ref[0, 0] * dk ** -0.5
    k = k_ref[0, 0]
    b = lg_ref[0, 0]
    v = v_ref[0, 0]
    state = s0_ref[0, 0]
    q_dec = q * jnp.exp(b)
    a = jnp.sum(q_dec * (k * jnp.exp(-b)), axis=0, keepdims=True)
    o = a * v + jnp.sum(q_dec * state, axis=0, keepdims=True)
    s_ref[0, 0] = jnp.exp(b) * state + (k * jnp.exp(b - b)) * v
    r = r_ref[0, 0]
    o_ref[0, 0] = _rmsnorm(o, gg_ref[...]) * (r * jax.nn.sigmoid(r))


def _gla_decode(q, k, lg, v, r, g_gla, s0, layer):
    _, b, heads, dk, dv = s0.shape
    col = pl.BlockSpec((1, 1, dk, 1), lambda i, h: (i, h, 0, 0))
    rowspec = pl.BlockSpec((1, 1, 1, dv), lambda i, h: (i, h, 0, 0))
    st = pl.BlockSpec((1, 1, dk, dv), lambda i, h: (i, h, 0, 0))
    return pl.pallas_call(
        functools.partial(_gla_decode_kernel, dk=dk),
        grid=(b, heads),
        in_specs=[col, col, col, rowspec, rowspec,
                  _layer_spec((1, dv), layer, lambda i, h: (0, 0)),
                  _layer_spec((1, 1, dk, dv), layer, lambda i, h: (i, h, 0, 0))],
        out_specs=[rowspec, st],
        out_shape=[jax.ShapeDtypeStruct((b, heads, 1, dv), F32),
                   jax.ShapeDtypeStruct((b, heads, dk, dv), s0.dtype)],
        compiler_params=_cparams("parallel", "parallel"),
        name="gla_decode",
    )(q, k, lg, v, r, _gain(g_gla), s0)


def _trunk(x, p, start, pool_state, gla_state, cache_k, cache_v, page_table, wts):
    b, t, d = x.shape
    m = b * t
    depth = wts["g_mix"].shape[0]
    pool_w = wts["pool_scale"].shape[1]
    pool_buf = max(POOL_WINDOWS) - 1
    heads_m, hd = cache_k.shape[3], cache_k.shape[4]
    moba_w = heads_m * hd
    _, _, heads_g, dk, dv = gla_state.shape
    dk_tot, dv_tot = heads_g * dk, heads_g * dv
    decode = page_table is not None
    assert pool_w == moba_w
    p = p.reshape(depth, m, -1)

    h = x.reshape(m, d)
    pools, ks, vs, glas = [], [], [], []
    for i in range(depth):
        j = i // 2
        if i % 2 == 0:
            u, q, k, v = (a.reshape(b, t, -1) for a in _norm_matmul(
                h, wts["g_mix"], wts["w_in_even"], i, j, pool_w + 3 * moba_w, n_out=4, tm=512, tn=1024))
            if decode:
                buf = pool_state[j]
                pools.append(jnp.concatenate([buf, u], axis=1)[:, -pool_buf:])
            else:
                buf = jnp.zeros((b, pool_buf, pool_w), x.dtype)
                pools.append(u[:, -pool_buf:])
            y_a = _pool_mixer(u, buf, start, wts["w_pool"], wts["pool_scale"], j)
            if decode:
                q3, k3, v3 = (a.reshape(b, heads_m, hd) for a in (q, k, v))
                ksum = _paged_block_sums(cache_k, j, page_table)
                sel = _decode_select(q3, ksum)
                y_b = _moba_decode(q3, k3, v3, cache_k, cache_v, j, page_table, sel)
            else:
                y_b = _moba_prompt(q, k, v)
            h = _proj_res([y_a.reshape(m, pool_w), y_b.reshape(m, moba_w)], wts["w_out_even"], j, h)
            ks.append(k.reshape(b, t, heads_m, hd))
            vs.append(v.reshape(b, t, heads_m, hd))
        else:
            n_main = 2 * dk_tot + 2 * dv_tot
            z = _norm_matmul(h, wts["g_mix"], wts["w_in_odd"], i, j, n_main)
            lg = _gla_gate(h, wts["g_mix"], wts["w_gate_odd"], wts["w_gk2"], wts["b_gk"], i, j)
            if decode:
                def cols(a):
                    return a.reshape(b, heads_g, dk, 1)
                def rows(a):
                    return a.reshape(b, heads_g, 1, dv)
                og, s_new = _gla_decode(
                    cols(z[:, :dk_tot]), cols(z[:, dk_tot:2 * dk_tot]), cols(lg),
                    rows(z[:, 2 * dk_tot:2 * dk_tot + dv_tot]), rows(z[:, 2 * dk_tot + dv_tot:]),
                    wts["g_gla"], gla_state, j)
            else:
                og, s_new = _gla_prompt(z.reshape(b, t, n_main), lg.reshape(b, t, dk_tot), wts["g_gla"], j,
                                        heads=heads_g, dk=dk, dv=dv)
            h = _proj_res([og.reshape(m, dv_tot)], wts["w_out_odd"], j, h)
            glas.append(s_new.astype(gla_state.dtype))
        h = _ffn(h, wts["g_ffn"], wts["w_ffn_gate"], wts["w_ffn_up"], wts["w_ffn_down"], i)
        h = _ple(h, wts["g_ple"], wts["w_ple_gate"], p, wts["w_ple_proj"], i)
    y = _final_norm(h, wts["g_final"]).reshape(b, t, d)
    return y, jnp.stack(pools), jnp.stack(ks), jnp.stack(vs), jnp.stack(glas)


_MATMUL_WEIGHTS = ("w_in_even", "w_pool", "w_out_even", "w_in_odd", "w_gk2", "w_out_odd",
                   "w_ffn_gate", "w_ffn_up", "w_ffn_down", "w_ple_gate", "w_ple_proj")


def kernel(x_prompt, x_sample, state_pool, cache_k, cache_v, state_gla, page_table, p_prompt, p_sample,
           g_mix, g_ffn, g_ple, g_final, w_in_even, w_pool, pool_scale, w_out_even,
           w_in_odd, w_gk2, b_gk, g_gla, w_out_odd, w_ffn_gate, w_ffn_up, w_ffn_down,
           w_ple_gate, w_ple_proj):
    wts = dict(g_mix=g_mix, g_ffn=g_ffn, g_ple=g_ple, g_final=g_final, w_in_even=w_in_even,
               w_pool=w_pool, pool_scale=pool_scale, w_out_even=w_out_even, w_in_odd=w_in_odd,
               w_gk2=w_gk2, b_gk=b_gk, g_gla=g_gla, w_out_odd=w_out_odd, w_ffn_gate=w_ffn_gate,
               w_ffn_up=w_ffn_up, w_ffn_down=w_ffn_down, w_ple_gate=w_ple_gate, w_ple_proj=w_ple_proj)
    n_main = 2 * state_gla.shape[2] * (state_gla.shape[3] + state_gla.shape[4])
    wts["w_gate_odd"] = w_in_odd[:, :, n_main:]
    wts_bf16 = dict(wts)
    for name in _MATMUL_WEIGHTS + ("w_gate_odd",):
        wts_bf16[name] = wts[name].astype(BF16)
    past_len = page_table.shape[1] * cache_k.shape[2]
    gla0 = jax.ShapeDtypeStruct((state_gla.shape[0], x_prompt.shape[0]) + state_gla.shape[2:], state_gla.dtype)
    y_p, pool_p, k_p, v_p, gla_p = _trunk(x_prompt, p_prompt, 0, None, gla0, cache_k, cache_v, None, wts_bf16)
    y_s, pool_s, k_s, v_s, gla_s = _trunk(x_sample, p_sample, past_len, state_pool, state_gla,
                                          cache_k, cache_v, page_table, wts)
    return (y_p, y_s, pool_p, pool_s, k_p, k_s, v_p, v_s, gla_p, gla_s)
```

```python
import functools

import jax
import jax.numpy as jnp
from jax import lax
from jax.experimental import pallas as pl
from jax.experimental.pallas import tpu as pltpu

F32 = jnp.float32
BF16 = jnp.bfloat16

RMS_EPS = 1e-6
POOL_WINDOWS = (2, 4, 8, 16)
POOL_HALO = 16
MOBA_BLOCK = 256
MOBA_TOPK = 3
MOBA_VARIANT_BLOCKS = 4
GLA_GATE_TAU = 16.0
GLA_CHUNK = 64
LANES = 128
SUBLANES = 8
VMEM_LIMIT = 52 * 1024 * 1024
NEG_INF = float("-inf")
MASKED = -1e30


def _cparams(*sem):
    return pltpu.CompilerParams(dimension_semantics=sem, vmem_limit_bytes=VMEM_LIMIT)


def _rmsnorm(x, g):
    ms = jnp.mean(x * x, axis=-1, keepdims=True)
    return x * lax.rsqrt(ms + RMS_EPS) * g


def _dot(a, b):
    return jnp.dot(a, b, preferred_element_type=F32)


def _wdot(a, w):
    if w.dtype == F32:
        return jnp.dot(a.astype(F32), w, precision=lax.Precision.HIGHEST, preferred_element_type=F32)
    return _dot(a.astype(BF16), w)


def _dot_nt(a, b):
    return lax.dot_general(a, b, (((1,), (1,)), ((), ())), preferred_element_type=F32)


def _dot_tn(a, b):
    return lax.dot_general(a, b, (((0,), (0,)), ((), ())), preferred_element_type=F32)


def _split2(a):
    hi = a.astype(BF16)
    lo = (a - hi.astype(F32)).astype(BF16)
    return hi, lo


def _split3(a):
    hi = a.astype(BF16)
    r = a - hi.astype(F32)
    mid = r.astype(BF16)
    lo = (r - mid.astype(F32)).astype(BF16)
    return hi, mid, lo


def _dot_nt_3pass(a, b):
    ah, al = _split2(a)
    bh, bl = _split2(b)
    return _dot_nt(ah, bh) + (_dot_nt(ah, bl) + _dot_nt(al, bh))


def _tile(m, t):
    t = min(m, t)
    assert m % t == 0, (m, t)
    return t


def _act_dtype(rows):
    return BF16 if rows % 16 == 0 else F32


def _layer_spec(tail, layer, index):
    return pl.BlockSpec((None,) + tuple(tail), lambda *a: (layer,) + tuple(index(*a)))


def _gain(g):
    return g.reshape(g.shape[0], 1, g.shape[1])


def _norm_matmul_kernel(x_ref, g_ref, w_ref, *refs, steps_per_out):
    o_refs, hn_ref = refs[:-1], refs[-1]
    j = pl.program_id(1)

    @pl.when(j == 0)
    def _():
        hn_ref[...] = _rmsnorm(x_ref[...], g_ref[...]).astype(hn_ref.dtype)

    res = _wdot(hn_ref[...], w_ref[...])
    if len(o_refs) == 1:
        o_refs[0][...] = res
    else:
        for idx, o_ref in enumerate(o_refs):
            @pl.when(j // steps_per_out == idx)
            def _(o_ref=o_ref):
                o_ref[...] = res


def _norm_matmul(x, g, w, layer, wlayer, n, *, n_out=1, tm=1024, tn=512):
    m, d = x.shape
    group = n // n_out
    tm, tn = _tile(m, tm), _tile(group, tn)
    spo = group // tn

    def out_index(idx):
        return lambda i, j: (i, jnp.clip(j - idx * spo, 0, spo - 1))

    outs = pl.pallas_call(
        functools.partial(_norm_matmul_kernel, steps_per_out=spo),
        grid=(m // tm, n // tn),
        in_specs=[pl.BlockSpec((tm, d), lambda i, j: (i, 0)),
                  _layer_spec((1, d), layer, lambda i, j: (0, 0)),
                  _layer_spec((d, tn), wlayer, lambda i, j: (0, j))],
        out_specs=[pl.BlockSpec((tm, tn), out_index(idx)) for idx in range(n_out)],
        out_shape=[jax.ShapeDtypeStruct((m, group), F32)] * n_out,
        scratch_shapes=[pltpu.VMEM((tm, d), _act_dtype(tm))],
        compiler_params=_cparams("parallel", "arbitrary"),
        name="norm_matmul",
    )(x, _gain(g), w)
    return outs[0] if n_out == 1 else outs


def _post_mixer_kernel(*refs, n_in, na, nf, tn):
    a_refs, wo_refs = refs[:n_in], refs[n_in:2 * n_in]
    (res_ref, gf_ref, wg_ref, wu_ref, wd_ref, gp_ref, wpg_ref, p_ref, wpp_ref,
     o_ref, h_ref, hn_ref) = refs[2 * n_in:]
    j = pl.program_id(1)
    nt = h_ref.shape[0]

    def norm_rows(g_ref):
        ss = jnp.sum(h_ref[0] * h_ref[0], axis=-1, keepdims=True)
        for t in range(1, nt):
            ss = ss + jnp.sum(h_ref[t] * h_ref[t], axis=-1, keepdims=True)
        rs = lax.rsqrt(ss / (nt * tn) + RMS_EPS)
        for t in range(nt):
            cols = slice(t * tn, (t + 1) * tn)
            hn_ref[:, cols] = (h_ref[t] * rs * g_ref[:, cols]).astype(hn_ref.dtype)

    @pl.when(j < na)
    def _():
        acc = res_ref[...]
        for a_ref, w_ref in zip(a_refs, wo_refs):
            acc = acc + _wdot(a_ref[...], w_ref[...])
        h_ref[j] = acc

    @pl.when(j == na)
    def _():
        norm_rows(gf_ref)

    @pl.when((j >= na) & (j < na + nf))
    def _():
        hn = hn_ref[...]
        a = _wdot(hn, wg_ref[...])
        b = _wdot(hn, wu_ref[...])
        y = _wdot((a * jax.nn.sigmoid(a)) * b, wd_ref[...])
        for t in range(nt):
            h_ref[t] += y[:, t * tn:(t + 1) * tn]

    @pl.when(j == na + nf)
    def _():
        norm_rows(gp_ref)

    @pl.when(j >= na + nf)
    def _():
        gate = _wdot(hn_ref[...], wpg_ref[...])
        proj = _wdot(p_ref[...], wpp_ref[...])
        o_ref[...] = h_ref[j - (na + nf)] + jax.nn.sigmoid(gate) * proj


def _post_mixer(a_list, w_out, olayer, res, wts, layer, p, *, tm=512, tn=512, tf=512):
    m, d = res.shape
    f = wts["w_ffn_gate"].shape[2]
    pd = p.shape[2]
    tm, tn, tf = _tile(m, tm), _tile(d, tn), _tile(f, tf)
    n_in, kw = len(a_list), a_list[0].shape[1]
    assert all(a.shape[1] == kw for a in a_list) and kw * n_in == w_out.shape[1]
    na, nf = d // tn, f // tf

    def ia(j):
        return jnp.minimum(j, na - 1)

    def ib(j):
        return jnp.clip(j - na, 0, nf - 1)

    def ic(j):
        return jnp.clip(j - na - nf, 0, na - 1)

    in_specs = (
        [pl.BlockSpec((tm, kw), lambda i, j: (i, 0)) for _ in a_list]
        + [_layer_spec((kw, tn), olayer, lambda i, j, r=r: (r, ia(j))) for r in range(n_in)]
        + [pl.BlockSpec((tm, tn), lambda i, j: (i, ia(j))),
           _layer_spec((1, d), layer, lambda i, j: (0, 0)),
           _layer_spec((d, tf), layer, lambda i, j: (0, ib(j))),
           _layer_spec((d, tf), layer, lambda i, j: (0, ib(j))),
           _layer_spec((tf, d), layer, lambda i, j: (ib(j), 0)),
           _layer_spec((1, d), layer, lambda i, j: (0, 0)),
           _layer_spec((d, tn), layer, lambda i, j: (0, ic(j))),
           _layer_spec((tm, pd), layer, lambda i, j: (i, 0)),
           _layer_spec((pd, tn), layer, lambda i, j: (0, ic(j)))])
    return pl.pallas_call(
        functools.partial(_post_mixer_kernel, n_in=n_in, na=na, nf=nf, tn=tn),
        grid=(m // tm, 2 * na + nf),
        in_specs=in_specs,
        out_specs=pl.BlockSpec((tm, tn), lambda i, j: (i, ic(j))),
        out_shape=jax.ShapeDtypeStruct((m, d), F32),
        scratch_shapes=[pltpu.VMEM((na, tm, tn), F32), pltpu.VMEM((tm, d), _act_dtype(tm))],
        compiler_params=_cparams("parallel", "arbitrary"),
        name="post_mixer",
    )(*a_list, *([w_out] * n_in), res, _gain(wts["g_ffn"]), wts["w_ffn_gate"], wts["w_ffn_up"],
      wts["w_ffn_down"], _gain(wts["g_ple"]), wts["w_ple_gate"], p, wts["w_ple_proj"])


def _final_norm_kernel(x_ref, g_ref, o_ref):
    o_ref[...] = _rmsnorm(x_ref[...], g_ref[...])


def _final_norm(x, g, *, tm=512):
    m, d = x.shape
    tm = _tile(m, tm)
    return pl.pallas_call(
        _final_norm_kernel,
        grid=(m // tm,),
        in_specs=[pl.BlockSpec((tm, d), lambda i: (i, 0)),
                  pl.BlockSpec((1, d), lambda i: (0, 0))],
        out_specs=pl.BlockSpec((tm, d), lambda i: (i, 0)),
        out_shape=jax.ShapeDtypeStruct((m, d), F32),
        compiler_params=_cparams("parallel"),
        name="final_norm",
    )(x, g.reshape(1, d))


def _pool_kernel(u_ref, halo_ref, w_ref, s_ref, o_ref, ext_ref, *, tt, start, gw):
    t = pl.program_id(1)

    @pl.when(t == 0)
    def _():
        ext_ref[0:POOL_HALO, :] = halo_ref[0]

    @pl.when(t > 0)
    def _():
        ext_ref[0:POOL_HALO, :] = ext_ref[tt:tt + POOL_HALO, :]

    ext_ref[POOL_HALO:POOL_HALO + tt, :] = u_ref[0]
    pos = start + t * tt + lax.broadcasted_iota(jnp.int32, (tt, 1), 0)
    for g, w in enumerate(POOL_WINDOWS):
        c0, c1 = g * gw, (g + 1) * gw
        cur = ext_ref[POOL_HALO:POOL_HALO + tt, c0:c1]
        acc = cur
        for k in range(1, w):
            acc = acc + ext_ref[POOL_HALO - k:POOL_HALO - k + tt, c0:c1]
        cnt = jnp.minimum(pos + 1, w).astype(F32)
        d = acc / cnt - cur
        y = _wdot(d, w_ref[g])
        o_ref[0, :, c0:c1] = (y * s_ref[:, c0:c1]).astype(o_ref.dtype)


def _pool_mixer(u, buf, start, w_pool, scale, layer, *, tt=512):
    b, t, width = u.shape
    tt = _tile(t, tt)
    gw = width // len(POOL_WINDOWS)
    halo = jnp.pad(buf, ((0, 0), (POOL_HALO - buf.shape[1], 0), (0, 0)))
    return pl.pallas_call(
        functools.partial(_pool_kernel, tt=tt, start=start, gw=gw),
        grid=(b, t // tt),
        in_specs=[pl.BlockSpec((1, tt, width), lambda i, j: (i, j, 0)),
                  pl.BlockSpec((1, POOL_HALO, width), lambda i, j: (i, 0, 0)),
                  _layer_spec(w_pool.shape[1:], layer, lambda i, j: (0, 0, 0)),
                  _layer_spec((1, width), layer, lambda i, j: (0, 0))],
        out_specs=pl.BlockSpec((1, tt, width), lambda i, j: (i, j, 0)),
        out_shape=jax.ShapeDtypeStruct((b, t, width), _act_dtype(tt)),
        scratch_shapes=[pltpu.VMEM((POOL_HALO + max(tt, POOL_HALO), width), F32)],
        compiler_params=_cparams("parallel", "arbitrary"),
        name="pool_mixer",
    )(u, halo, w_pool, _gain(scale))


def _moba_prompt_kernel(q_ref, k_ref, v_ref, o_ref, kmean_ref, kaug_ref, kaug_t_ref, vaug_ref, aug_ref,
                        *, nb, nbp, hd, heads):
    h = pl.program_id(1)
    qi = pl.program_id(2)
    blk = MOBA_BLOCK
    slope = jnp.exp2(-jnp.full((1, 1), h + 1, jnp.int32).astype(F32) * (8.0 / heads))
    lane = lax.broadcasted_iota(jnp.int32, (blk, LANES), 1)

    @pl.when(qi == 0)
    def _():
        kmean_ref[...] = jnp.zeros_like(kmean_ref)
        offs = lax.broadcasted_iota(jnp.int32, (blk, LANES), 0).astype(F32)
        ones_col = jnp.where(lane == 0, 1.0, 0.0).astype(BF16)
        for n in range(nb):
            rows = slice(n * blk, (n + 1) * blk)
            kb = k_ref[0, rows, :]
            kmean_ref[n:n + 1, :] = jnp.sum(kb, axis=0, keepdims=True) / blk
            ext = jnp.where(lane == n, 1.0, 0.0)
            ext = jnp.where((lane >= nbp) & (lane < nbp + 3), float(n * blk), ext)
            ext = jnp.where((lane >= nbp + 3) & (lane < nbp + 6), offs, ext)
            kaug_ref[rows, 0:hd] = kb.astype(BF16)
            kaug_ref[rows, hd:] = ext.astype(BF16)
            kaug_t_ref[0:hd, rows] = kb.T.astype(BF16)
            kaug_t_ref[hd:, rows] = ext.T.astype(BF16)
            vaug_ref[rows, 0:hd] = v_ref[0, rows, :].astype(BF16)
            vaug_ref[rows, hd:] = ones_col

    q = q_ref[0]

    bid = lax.broadcasted_iota(jnp.int32, (nbp, blk), 0)
    bid_f = bid.astype(F32)
    fully_past = bid < qi
    gate = jnp.where(fully_past, _dot_nt_3pass(kmean_ref[...], q), NEG_INF)
    sel = jnp.zeros((nbp, blk), jnp.bool_)
    for _ in range(MOBA_TOPK):
        m = jnp.max(gate, axis=0, keepdims=True)
        first = jnp.min(jnp.where(gate == m, bid_f, float(nbp)), axis=0, keepdims=True)
        pick = bid_f == first
        sel = sel | pick
        gate = jnp.where(pick, NEG_INF, gate)

    s_hi, s_mid, s_lo = (p.astype(F32) for p in _split3(slope))
    part = lax.broadcasted_iota(jnp.int32, (SUBLANES, blk), 0)
    slope_rows = jnp.where(part % 3 == 0, s_hi, jnp.where(part % 3 == 1, s_mid, s_lo))
    slope_rows = jnp.where(part < 6, slope_rows, 0.0)
    aug_ref[...] = jnp.zeros_like(aug_ref)
    aug_ref[0:nbp, :] = jnp.where(sel & fully_past, 0.0, MASKED)
    aug_ref[nbp:nbp + SUBLANES, :] = slope_rows
    q_ext = aug_ref[...].T
    qs = (q * hd ** -0.5).astype(BF16)
    q_aug = jnp.concatenate([qs, q_ext.astype(BF16)], axis=1)
    q_own = jnp.concatenate([qs, jnp.where(lane < nbp, 0.0, q_ext).astype(BF16)], axis=1)

    own = pl.ds(pl.multiple_of(qi * blk, blk), blk)
    k_own = kaug_ref[own, :]
    v_own = vaug_ref[own, :]
    row = lax.broadcasted_iota(jnp.int32, (blk, blk), 0)
    col = lax.broadcasted_iota(jnp.int32, (blk, blk), 1)
    s_own = jnp.where(col <= row, _dot_nt(q_own, k_own), MASKED)
    m_own = jnp.max(s_own, axis=-1, keepdims=True)

    def finish(n_past):
        s_blocks = [_dot(q_aug, kaug_t_ref[:, n * blk:(n + 1) * blk]) for n in range(n_past)]
        m_run = s_own
        for s_n in s_blocks:
            m_run = jnp.maximum(m_run, s_n)
        m = jnp.max(m_run, axis=-1, keepdims=True)
        acc = _dot(jnp.exp(s_own - m).astype(BF16), v_own)
        for n, s_n in enumerate(s_blocks):
            acc = acc + _dot(jnp.exp(s_n - m).astype(BF16), vaug_ref[n * blk:(n + 1) * blk, :])
        o_ref[0] = (acc[:, 0:hd] / acc[:, hd:hd + 1]).astype(o_ref.dtype)

    @pl.when(qi == 0)
    def _():
        finish(0)

    lo = 0
    while lo < nb - 1:
        hi = min(lo + MOBA_VARIANT_BLOCKS, nb - 1)

        @pl.when((qi > lo) & (qi <= hi))
        def _(hi=hi):
            finish(hi)

        lo = hi


def _moba_prompt(q, k, v):
    b, t, width = q.shape
    blk, hd = MOBA_BLOCK, LANES
    heads = width // hd
    assert t % blk == 0 and width % hd == 0
    nb = t // blk
    nbp = -(-nb // SUBLANES) * SUBLANES
    assert nb >= MOBA_TOPK and nbp + SUBLANES <= LANES
    return pl.pallas_call(
        functools.partial(_moba_prompt_kernel, nb=nb, nbp=nbp, hd=hd, heads=heads),
        grid=(b, heads, nb),
        in_specs=[pl.BlockSpec((1, blk, hd), lambda i, h, j: (i, j, h)),
                  pl.BlockSpec((1, t, hd), lambda i, h, j: (i, 0, h)),
                  pl.BlockSpec((1, t, hd), lambda i, h, j: (i, 0, h))],
        out_specs=pl.BlockSpec((1, blk, hd), lambda i, h, j: (i, j, h)),
        out_shape=jax.ShapeDtypeStruct((b, t, width), BF16),
        scratch_shapes=[pltpu.VMEM((nbp, hd), F32),
                        pltpu.VMEM((t, 2 * hd), BF16),
                        pltpu.VMEM((2 * hd, t), BF16),
                        pltpu.VMEM((t, 2 * hd), BF16),
                        pltpu.VMEM((LANES, blk), F32)],
        compiler_params=_cparams("parallel", "parallel", "arbitrary"),
        name="moba_prompt",
    )(q, k, v)


PAGES_PER_STEP = 16


def _block_sum_kernel(pt_ref, *refs, page, blk):
    del pt_ref
    k_refs, o_ref = refs[:-1], refs[-1]
    per_blk = blk // page
    for n in range(len(k_refs) // per_blk):
        acc = jnp.sum(k_refs[n * per_blk][...], axis=0)
        for r in range(1, per_blk):
            acc = acc + jnp.sum(k_refs[n * per_blk + r][...], axis=0)
        o_ref[0, n] = acc


def _paged_block_sums(cache, layer, page_table):
    _, _, page, heads, hd = cache.shape
    b, n_pages = page_table.shape
    pps = PAGES_PER_STEP
    assert MOBA_BLOCK % page == 0 and (pps * page) % MOBA_BLOCK == 0 and n_pages % pps == 0
    blocks_per_step = pps * page // MOBA_BLOCK

    def page_spec(r):
        return pl.BlockSpec((None, None, page, heads, hd),
                            lambda i, s, pt: (layer, pt[i, s * pps + r], 0, 0, 0))

    return pl.pallas_call(
        functools.partial(_block_sum_kernel, page=page, blk=MOBA_BLOCK),
        grid_spec=pltpu.PrefetchScalarGridSpec(
            num_scalar_prefetch=1,
            grid=(b, n_pages // pps),
            in_specs=[page_spec(r) for r in range(pps)],
            out_specs=pl.BlockSpec((1, blocks_per_step, heads, hd), lambda i, s, pt: (i, s, 0, 0)),
        ),
        out_shape=jax.ShapeDtypeStruct((b, n_pages * page // MOBA_BLOCK, heads, hd), F32),
        compiler_params=_cparams("parallel", "arbitrary"),
        name="paged_block_sums",
    )(page_table, *([cache] * pps))


def _decode_select_kernel(q_ref, ks_ref, o_ref, *, nb):
    gate = jnp.sum(ks_ref[0] * q_ref[0][None], axis=-1, keepdims=True) / MOBA_BLOCK
    idx = lax.broadcasted_iota(jnp.int32, gate.shape, 0)
    for r in range(MOBA_TOPK):
        m = jnp.max(gate, axis=0, keepdims=True)
        first = jnp.min(jnp.where(gate == m, idx, nb), axis=0, keepdims=True)
        o_ref[0, r] = jnp.broadcast_to(first[0], o_ref.shape[2:])
        gate = jnp.where(idx == first, NEG_INF, gate)


def _decode_select(q, ksum):
    b, nb, heads, hd = ksum.shape
    assert nb >= MOBA_TOPK
    out = pl.pallas_call(
        functools.partial(_decode_select_kernel, nb=nb),
        grid=(b,),
        in_specs=[pl.BlockSpec((1, heads, hd), lambda i: (i, 0, 0)),
                  pl.BlockSpec((1, nb, heads, hd), lambda i: (i, 0, 0, 0))],
        out_specs=pl.BlockSpec((1, MOBA_TOPK, heads, LANES), lambda i: (i, 0, 0, 0)),
        out_shape=jax.ShapeDtypeStruct((b, MOBA_TOPK, heads, LANES), jnp.int32),
        compiler_params=_cparams("parallel"),
        name="decode_select",
    )(q, ksum)
    return out[..., 0]


def _decode_attn_kernel(pt_ref, sel_ref, q_ref, kn_ref, vn_ref, *refs,
                        page, heads, past_len, per_blk):
    del pt_ref
    k_refs, v_refs = refs[:per_blk], refs[per_blk:2 * per_blk]
    o_ref, m_ref, l_ref, acc_ref = refs[2 * per_blk:]
    i, h, j = pl.program_id(0), pl.program_id(1), pl.program_id(2)
    hd = q_ref.shape[2]
    q = q_ref[0, pl.ds(h, 1), :]
    scale = hd ** -0.5
    slope = jnp.exp2(-jnp.full((1, 1), h + 1, jnp.int32).astype(F32) * (8.0 / heads))

    @pl.when(j == 0)
    def _():
        m_ref[...] = jnp.sum(q * kn_ref[0, pl.ds(h, 1), :], axis=-1, keepdims=True) * scale
        l_ref[...] = jnp.ones_like(l_ref)
        acc_ref[...] = vn_ref[0, pl.ds(h, 1), :]

    n = sel_ref[(i * MOBA_TOPK + j) * heads + h]
    r = lax.broadcasted_iota(jnp.int32, (page, 1), 0)
    for part in range(per_blk):
        k_h = k_refs[part][pl.ds(h, page, stride=heads), :]
        v_h = v_refs[part][pl.ds(h, page, stride=heads), :]
        kpos = n * MOBA_BLOCK + part * page + r
        s = jnp.sum(k_h * q, axis=-1, keepdims=True) * scale - slope * (past_len - kpos).astype(F32)
        m_old = m_ref[...]
        m_new = jnp.maximum(m_old, jnp.max(s, axis=0, keepdims=True))
        alpha = jnp.exp(m_old - m_new)
        p = jnp.exp(s - m_new)
        l_ref[...] = alpha * l_ref[...] + jnp.sum(p, axis=0, keepdims=True)
        acc_ref[...] = alpha * acc_ref[...] + jnp.sum(p * v_h, axis=0, keepdims=True)
        m_ref[...] = m_new

    @pl.when(j == MOBA_TOPK - 1)
    def _():
        o_ref[0, pl.ds(h, 1), :] = acc_ref[...] / l_ref[...]


def _moba_decode(q, k_new, v_new, cache_k, cache_v, layer, page_table, sel):
    b, heads, hd = q.shape
    n_layers, n_pool, page = cache_k.shape[:3]
    per_blk = MOBA_BLOCK // page
    past_len = page_table.shape[1] * page
    assert past_len % MOBA_BLOCK == 0
    rows = page * heads
    cache_k = cache_k.reshape(n_layers, n_pool, rows, hd)
    cache_v = cache_v.reshape(n_layers, n_pool, rows, hd)

    def page_spec(part):
        def index(i, h, j, pt, sl):
            n = sl[(i * MOBA_TOPK + j) * heads + h]
            return (layer, pt[i, n * per_blk + part], 0, 0)
        return pl.BlockSpec((None, None, rows, hd), index)

    tok_spec = pl.BlockSpec((1, heads, hd), lambda i, h, j, pt, sl: (i, 0, 0))
    return pl.pallas_call(
        functools.partial(_decode_attn_kernel, page=page, heads=heads, past_len=past_len, per_blk=per_blk),
        grid_spec=pltpu.PrefetchScalarGridSpec(
            num_scalar_prefetch=2,
            grid=(b, heads, MOBA_TOPK),
            in_specs=[tok_spec, tok_spec, tok_spec]
                     + [page_spec(r) for r in range(per_blk)] * 2,
            out_specs=tok_spec,
            scratch_shapes=[pltpu.VMEM((1, 1), F32), pltpu.VMEM((1, 1), F32),
                            pltpu.VMEM((1, hd), F32)],
        ),
        out_shape=jax.ShapeDtypeStruct((b, heads, hd), F32),
        compiler_params=_cparams("parallel", "arbitrary", "arbitrary"),
        name="moba_decode",
    )(page_table, sel.reshape(-1), q, k_new, v_new,
      *([cache_k] * per_blk), *([cache_v] * per_blk))


def _gla_gate_kernel(x_ref, g_ref, wa_ref, w2_ref, b_ref, o_ref):
    hn = _rmsnorm(x_ref[...], g_ref[...])
    a = _wdot(hn, wa_ref[...])
    z = _wdot(a, w2_ref[...]) + b_ref[...]
    o_ref[...] = jax.nn.log_sigmoid(z) / GLA_GATE_TAU


def _gla_gate(x, g, wa, w2, bias, layer, wlayer, *, tm=512):
    m, d = x.shape
    rank, n = w2.shape[1:]
    tm = _tile(m, tm)
    return pl.pallas_call(
        _gla_gate_kernel,
        grid=(m // tm,),
        in_specs=[pl.BlockSpec((tm, d), lambda i: (i, 0)),
                  _layer_spec((1, d), layer, lambda i: (0, 0)),
                  _layer_spec((d, rank), wlayer, lambda i: (0, 0)),
                  _layer_spec((rank, n), wlayer, lambda i: (0, 0)),
                  _layer_spec((1, n), wlayer, lambda i: (0, 0))],
        out_specs=pl.BlockSpec((tm, n), lambda i: (i, 0)),
        out_shape=jax.ShapeDtypeStruct((m, n), F32),
        compiler_params=_cparams("parallel"),
        name="gla_gate",
    )(x, _gain(g), wa, w2, _gain(bias))


def _gla_prompt_kernel(z_ref, lg_ref, gg_ref, o_ref, s_ref, *, heads, dk, dv):
    c = GLA_CHUNK

    @pl.when(pl.program_id(0) == 0)
    def _():
        s_ref[...] = jnp.zeros_like(s_ref)

    row = lax.broadcasted_iota(jnp.int32, (c, c), 0)
    col = lax.broadcasted_iota(jnp.int32, (c, c), 1)
    causal = col <= row
    tri = jnp.where(causal, 1.0, 0.0).astype(BF16)
    k0, v0, r0 = heads * dk, 2 * heads * dk, 2 * heads * dk + heads * dv
    for bi in range(z_ref.shape[0]):
        for h in range(heads):
            q = z_ref[bi, :, h * dk:(h + 1) * dk] * dk ** -0.5
            k = z_ref[bi, :, k0 + h * dk:k0 + (h + 1) * dk]
            v = z_ref[bi, :, v0 + h * dv:v0 + (h + 1) * dv].astype(BF16)
            lg_hi, lg_mid, lg_lo = _split3(lg_ref[bi, :, h * dk:(h + 1) * dk])
            b = _dot(tri, lg_hi) + (_dot(tri, lg_mid) + _dot(tri, lg_lo))
            b_last = b[c - 1:c, :]
            q_dec = (q * jnp.exp(b)).astype(BF16)
            a = _dot_nt(q_dec, (k * jnp.exp(-b)).astype(BF16))
            a = jnp.where(causal, a, 0.0)
            state = s_ref[bi, h]
            o = _dot(a.astype(BF16), v) + _dot(q_dec, state.astype(BF16))
            k_rem = (k * jnp.exp(b_last - b)).astype(BF16)
            decay = jnp.exp(jnp.broadcast_to(b_last, (LANES, dk))).T[:, 0:1]
            s_ref[bi, h] = decay * state + _dot_tn(k_rem, v)
            r = z_ref[bi, :, r0 + h * dv:r0 + (h + 1) * dv]
            o_ref[bi, :, h * dv:(h + 1) * dv] = (
                _rmsnorm(o, gg_ref[...]) * (r * jax.nn.sigmoid(r))).astype(o_ref.dtype)


def _gla_prompt(z, lg, g_gla, layer, *, heads, dk, dv):
    b, t, zw = z.shape
    c = GLA_CHUNK
    assert t % c == 0 and zw == 2 * heads * (dk + dv)
    return pl.pallas_call(
        functools.partial(_gla_prompt_kernel, heads=heads, dk=dk, dv=dv),
        grid=(t // c,),
        in_specs=[pl.BlockSpec((b, c, zw), lambda j: (0, j, 0)),
                  pl.BlockSpec((b, c, heads * dk), lambda j: (0, j, 0)),
                  _layer_spec((1, dv), layer, lambda j: (0, 0))],
        out_specs=[pl.BlockSpec((b, c, heads * dv), lambda j: (0, j, 0)),
                   pl.BlockSpec((b, heads, dk, dv), lambda j: (0, 0, 0, 0))],
        out_shape=[jax.ShapeDtypeStruct((b, t, heads * dv), BF16),
                   jax.ShapeDtypeStruct((b, heads, dk, dv), F32)],
        compiler_params=_cparams("arbitrary"),
        name="gla_prompt",
    )(z, lg, _gain(g_gla))


def _gla_decode_kernel(q_ref, k_ref, lg_ref, v_ref, r_ref, gg_ref, s0_ref, o_ref, s_ref, *, dk):
    q = q_ref[0, 0] * dk ** -0.5
    k = k_ref[0, 0]
    b = lg_ref[0, 0]
    v = v_ref[0, 0]
    state = s0_ref[0, 0]
    q_dec = q * jnp.exp(b)
    a = jnp.sum(q_dec * (k * jnp.exp(-b)), axis=0, keepdims=True)
    o = a * v + jnp.sum(q_dec * state, axis=0, keepdims=True)
    s_ref[0, 0] = jnp.exp(b) * state + (k * jnp.exp(b - b)) * v
    r = r_ref[0, 0]
    o_ref[0, 0] = _rmsnorm(o, gg_ref[...]) * (r * jax.nn.sigmoid(r))


def _gla_decode(q, k, lg, v, r, g_gla, s0, layer):
    _, b, heads, dk, dv = s0.shape
    col = pl.BlockSpec((1, 1, dk, 1), lambda i, h: (i, h, 0, 0))
    rowspec = pl.BlockSpec((1, 1, 1, dv), lambda i, h: (i, h, 0, 0))
    st = pl.BlockSpec((1, 1, dk, dv), lambda i, h: (i, h, 0, 0))
    return pl.pallas_call(
        functools.partial(_gla_decode_kernel, dk=dk),
        grid=(b, heads),
        in_specs=[col, col, col, rowspec, rowspec,
                  _layer_spec((1, dv), layer, lambda i, h: (0, 0)),
                  _layer_spec((1, 1, dk, dv), layer, lambda i, h: (i, h, 0, 0))],
        out_specs=[rowspec, st],
        out_shape=[jax.ShapeDtypeStruct((b, heads, 1, dv), F32),
                   jax.ShapeDtypeStruct((b, heads, dk, dv), s0.dtype)],
        compiler_params=_cparams("parallel", "parallel"),
        name="gla_decode",
    )(q, k, lg, v, r, _gain(g_gla), s0)


def _trunk(x, p, start, pool_state, gla_state, cache_k, cache_v, page_table, wts):
    b, t, d = x.shape
    m = b * t
    depth = wts["g_mix"].shape[0]
    pool_w = wts["pool_scale"].shape[1]
    pool_buf = max(POOL_WINDOWS) - 1
    heads_m, hd = cache_k.shape[3], cache_k.shape[4]
    moba_w = heads_m * hd
    _, _, heads_g, dk, dv = gla_state.shape
    dk_tot, dv_tot = heads_g * dk, heads_g * dv
    decode = page_table is not None
    assert pool_w == moba_w
    p = p.reshape(depth, m, -1)

    h = x.reshape(m, d)
    pools, ks, vs, glas = [], [], [], []
    for i in range(depth):
        j = i // 2
        if i % 2 == 0:
            u, q, k, v = (a.reshape(b, t, -1) for a in _norm_matmul(
                h, wts["g_mix"], wts["w_in_even"], i, j, pool_w + 3 * moba_w, n_out=4))
            if decode:
                buf = pool_state[j]
                pools.append(jnp.concatenate([buf, u], axis=1)[:, -pool_buf:])
            else:
                buf = jnp.zeros((b, pool_buf, pool_w), x.dtype)
                pools.append(u[:, -pool_buf:])
            y_a = _pool_mixer(u, buf, start, wts["w_pool"], wts["pool_scale"], j)
            if decode:
                q3, k3, v3 = (a.reshape(b, heads_m, hd) for a in (q, k, v))
                ksum = _paged_block_sums(cache_k, j, page_table)
                sel = _decode_select(q3, ksum)
                y_b = _moba_decode(q3, k3, v3, cache_k, cache_v, j, page_table, sel)
            else:
                y_b = _moba_prompt(q, k, v)
            mixed, w_out = [y_a.reshape(m, pool_w), y_b.reshape(m, moba_w)], wts["w_out_even"]
            ks.append(k.reshape(b, t, heads_m, hd))
            vs.append(v.reshape(b, t, heads_m, hd))
        else:
            n_main = 2 * dk_tot + 2 * dv_tot
            z = _norm_matmul(h, wts["g_mix"], wts["w_in_odd"], i, j, n_main)
            lg = _gla_gate(h, wts["g_mix"], wts["w_gate_odd"], wts["w_gk2"], wts["b_gk"], i, j)
            if decode:
                def cols(a):
                    return a.reshape(b, heads_g, dk, 1)
                def rows(a):
                    return a.reshape(b, heads_g, 1, dv)
                og, s_new = _gla_decode(
                    cols(z[:, :dk_tot]), cols(z[:, dk_tot:2 * dk_tot]), cols(lg),
                    rows(z[:, 2 * dk_tot:2 * dk_tot + dv_tot]), rows(z[:, 2 * dk_tot + dv_tot:]),
                    wts["g_gla"], gla_state, j)
            else:
                og, s_new = _gla_prompt(z.reshape(b, t, n_main), lg.reshape(b, t, dk_tot), wts["g_gla"], j,
                                        heads=heads_g, dk=dk, dv=dv)
            mixed, w_out = [og.reshape(m, dv_tot)], wts["w_out_odd"]
            glas.append(s_new.astype(gla_state.dtype))
        h = _post_mixer(mixed, w_out, j, h, wts, i, p)
    y = _final_norm(h, wts["g_final"]).reshape(b, t, d)
    return y, jnp.stack(pools), jnp.stack(ks), jnp.stack(vs), jnp.stack(glas)


_MATMUL_WEIGHTS = ("w_in_even", "w_pool", "w_out_even", "w_in_odd", "w_gk2", "w_out_odd",
                   "w_ffn_gate", "w_ffn_up", "w_ffn_down", "w_ple_gate", "w_ple_proj")


def kernel(x_prompt, x_sample, state_pool, cache_k, cache_v, state_gla, page_table, p_prompt, p_sample,
           g_mix, g_ffn, g_ple, g_final, w_in_even, w_pool, pool_scale, w_out_even,
           w_in_odd, w_gk2, b_gk, g_gla, w_out_odd, w_ffn_gate, w_ffn_up, w_ffn_down,
           w_ple_gate, w_ple_proj):
    wts = dict(g_mix=g_mix, g_ffn=g_ffn, g_ple=g_ple, g_final=g_final, w_in_even=w_in_even,
               w_pool=w_pool, pool_scale=pool_scale, w_out_even=w_out_even, w_in_odd=w_in_odd,
               w_gk2=w_gk2, b_gk=b_gk, g_gla=g_gla, w_out_odd=w_out_odd, w_ffn_gate=w_ffn_gate,
               w_ffn_up=w_ffn_up, w_ffn_down=w_ffn_down, w_ple_gate=w_ple_gate, w_ple_proj=w_ple_proj)
    n_main = 2 * state_gla.shape[2] * (state_gla.shape[3] + state_gla.shape[4])
    wts["w_in_odd"] = w_in_odd[:, :, :n_main]
    wts["w_gate_odd"] = w_in_odd[:, :, n_main:]
    wts_bf16 = dict(wts)
    for name in _MATMUL_WEIGHTS + ("w_gate_odd",):
        wts_bf16[name] = wts[name].astype(BF16)
    past_len = page_table.shape[1] * cache_k.shape[2]
    gla0 = jax.ShapeDtypeStruct((state_gla.shape[0], x_prompt.shape[0]) + state_gla.shape[2:], state_gla.dtype)
    y_p, pool_p, k_p, v_p, gla_p = _trunk(x_prompt, p_prompt, 0, None, gla0, cache_k, cache_v, None, wts_bf16)
    y_s, pool_s, k_s, v_s, gla_s = _trunk(x_sample, p_sample, past_len, state_pool, state_gla,
                                          cache_k, cache_v, page_table, wts)
    return (y_p, y_s, pool_p, pool_s, k_p, k_s, v_p, v_s, gla_p, gla_s)
```

```python
import functools

import jax
import jax.numpy as jnp
from jax import lax
from jax.experimental import pallas as pl
from jax.experimental.pallas import tpu as pltpu

F32 = jnp.float32
BF16 = jnp.bfloat16

RMS_EPS = 1e-6
POOL_WINDOWS = (2, 4, 8, 16)
POOL_HALO = 16
MOBA_BLOCK = 256
MOBA_TOPK = 3
MOBA_VARIANT_BLOCKS = 4
MOBA_HEAD_GROUP = 2
GLA_GATE_TAU = 16.0
GLA_CHUNK = 64
LANES = 128
SUBLANES = 8
VMEM_LIMIT = 52 * 1024 * 1024
NEG_INF = float("-inf")
MASKED = -1e30


def _cparams(*sem):
    return pltpu.CompilerParams(dimension_semantics=sem, vmem_limit_bytes=VMEM_LIMIT)


def _rmsnorm(x, g):
    ms = jnp.mean(x * x, axis=-1, keepdims=True)
    return x * lax.rsqrt(ms + RMS_EPS) * g


def _dot(a, b):
    return jnp.dot(a, b, preferred_element_type=F32)


def _wdot(a, w, w_rows=False):
    dims = (((1,), (1 if w_rows else 0,)), ((), ()))
    if w.dtype == F32:
        return lax.dot_general(a.astype(F32), w, dims, precision=lax.Precision.HIGHEST,
                               preferred_element_type=F32)
    return lax.dot_general(a.astype(BF16), w, dims, preferred_element_type=F32)


def _dot_nt(a, b):
    return lax.dot_general(a, b, (((1,), (1,)), ((), ())), preferred_element_type=F32)


def _dot_tn(a, b):
    return lax.dot_general(a, b, (((0,), (0,)), ((), ())), preferred_element_type=F32)


def _split2(a):
    hi = a.astype(BF16)
    lo = (a - hi.astype(F32)).astype(BF16)
    return hi, lo


def _split3(a):
    hi = a.astype(BF16)
    r = a - hi.astype(F32)
    mid = r.astype(BF16)
    lo = (r - mid.astype(F32)).astype(BF16)
    return hi, mid, lo


def _dot_nt_3pass(a, b):
    ah, al = _split2(a)
    bh, bl = _split2(b)
    return _dot_nt(ah, bh) + (_dot_nt(ah, bl) + _dot_nt(al, bh))


def _tile(m, t):
    t = min(m, t)
    assert m % t == 0, (m, t)
    return t


def _act_dtype(rows):
    return BF16 if rows % 16 == 0 else F32


def _layer_spec(tail, layer, index):
    return pl.BlockSpec((None,) + tuple(tail), lambda *a: (layer,) + tuple(index(*a)))


def _gain(g):
    return g.reshape(g.shape[0], 1, g.shape[1])


def _norm_matmul_kernel(x_ref, g_ref, w_ref, *refs, steps_per_out, w_rows):
    o_refs, hn_ref = refs[:-1], refs[-1]
    j = pl.program_id(1)

    @pl.when(j == 0)
    def _():
        hn_ref[...] = _rmsnorm(x_ref[...], g_ref[...]).astype(hn_ref.dtype)

    res = _wdot(hn_ref[...], w_ref[...], w_rows)
    if len(o_refs) == 1:
        o_refs[0][...] = res
    else:
        for idx, o_ref in enumerate(o_refs):
            @pl.when(j // steps_per_out == idx)
            def _(o_ref=o_ref):
                o_ref[...] = res


def _norm_matmul(x, g, w, layer, wlayer, n, *, n_out=1, w_rows=False, tm=1024, tn=512):
    m, d = x.shape
    group = n // n_out
    tm, tn = _tile(m, tm), _tile(group, tn)
    spo = group // tn

    def out_index(idx):
        return lambda i, j: (i, jnp.clip(j - idx * spo, 0, spo - 1))

    outs = pl.pallas_call(
        functools.partial(_norm_matmul_kernel, steps_per_out=spo, w_rows=w_rows),
        grid=(m // tm, n // tn),
        in_specs=[pl.BlockSpec((tm, d), lambda i, j: (i, 0)),
                  _layer_spec((1, d), layer, lambda i, j: (0, 0)),
                  (_layer_spec((tn, d), wlayer, lambda i, j: (j, 0)) if w_rows else
                   _layer_spec((d, tn), wlayer, lambda i, j: (0, j)))],
        out_specs=[pl.BlockSpec((tm, tn), out_index(idx)) for idx in range(n_out)],
        out_shape=[jax.ShapeDtypeStruct((m, group), F32)] * n_out,
        scratch_shapes=[pltpu.VMEM((tm, d), _act_dtype(tm))],
        compiler_params=_cparams("parallel", "arbitrary"),
        name="norm_matmul",
    )(x, _gain(g), w)
    return outs[0] if n_out == 1 else outs


def _proj_res_kernel(*refs, n_in):
    a_refs, w_refs = refs[:n_in], refs[n_in:2 * n_in]
    res_ref, o_ref = refs[2 * n_in], refs[2 * n_in + 1]
    acc = res_ref[...]
    for a_ref, w_ref in zip(a_refs, w_refs):
        acc = acc + _wdot(a_ref[...], w_ref[...])
    o_ref[...] = acc


def _proj_res(a_list, w, layer, res, *, tm=1024, tn=512):
    m, n = res.shape
    tm, tn = _tile(m, tm), _tile(n, tn)
    n_in = len(a_list)
    kw = a_list[0].shape[1]
    assert all(a.shape[1] == kw for a in a_list) and kw * n_in == w.shape[1]
    in_specs = ([pl.BlockSpec((tm, kw), lambda i, j: (i, 0)) for _ in a_list]
                + [_layer_spec((kw, tn), layer, lambda i, j, r=r: (r, j)) for r in range(n_in)]
                + [pl.BlockSpec((tm, tn), lambda i, j: (i, j))])
    return pl.pallas_call(
        functools.partial(_proj_res_kernel, n_in=n_in),
        grid=(m // tm, n // tn),
        in_specs=in_specs,
        out_specs=pl.BlockSpec((tm, tn), lambda i, j: (i, j)),
        out_shape=jax.ShapeDtypeStruct((m, n), F32),
        compiler_params=_cparams("parallel", "arbitrary"),
        name="proj_res",
    )(*a_list, *([w] * n_in), res)


def _ffn_kernel(x_ref, g_ref, wg_ref, wu_ref, wd_ref, o_ref, hn_ref):
    @pl.when(pl.program_id(1) == 0)
    def _():
        x = x_ref[...]
        hn_ref[...] = _rmsnorm(x, g_ref[...]).astype(hn_ref.dtype)
        o_ref[...] = x

    hn = hn_ref[...]
    a = _wdot(hn, wg_ref[...])
    b = _wdot(hn, wu_ref[...])
    h = (a * jax.nn.sigmoid(a)) * b
    o_ref[...] += _wdot(h, wd_ref[...])


def _ffn(x, g, wg, wu, wd, layer, *, tm=512, tf=512):
    m, d = x.shape
    f = wg.shape[2]
    tm, tf = _tile(m, tm), _tile(f, tf)
    return pl.pallas_call(
        _ffn_kernel,
        grid=(m // tm, f // tf),
        in_specs=[pl.BlockSpec((tm, d), lambda i, j: (i, 0)),
                  _layer_spec((1, d), layer, lambda i, j: (0, 0)),
                  _layer_spec((d, tf), layer, lambda i, j: (0, j)),
                  _layer_spec((d, tf), layer, lambda i, j: (0, j)),
                  _layer_spec((tf, d), layer, lambda i, j: (j, 0))],
        out_specs=pl.BlockSpec((tm, d), lambda i, j: (i, 0)),
        out_shape=jax.ShapeDtypeStruct((m, d), F32),
        scratch_shapes=[pltpu.VMEM((tm, d), _act_dtype(tm))],
        compiler_params=_cparams("parallel", "arbitrary"),
        name="ffn",
    )(x, _gain(g), wg, wu, wd)


def _ple_kernel(x_ref, xj_ref, g_ref, wg_ref, p_ref, wp_ref, o_ref, hn_ref):
    @pl.when(pl.program_id(1) == 0)
    def _():
        hn_ref[...] = _rmsnorm(x_ref[...], g_ref[...]).astype(hn_ref.dtype)

    gate = _wdot(hn_ref[...], wg_ref[...])
    proj = _wdot(p_ref[...], wp_ref[...])
    o_ref[...] = xj_ref[...] + jax.nn.sigmoid(gate) * proj


def _ple(x, g, wg, p, wp, layer, *, tm=1024, tn=512):
    m, d = x.shape
    tm, tn = _tile(m, tm), _tile(d, tn)
    pd = p.shape[2]
    return pl.pallas_call(
        _ple_kernel,
        grid=(m // tm, d // tn),
        in_specs=[pl.BlockSpec((tm, d), lambda i, j: (i, 0)),
                  pl.BlockSpec((tm, tn), lambda i, j: (i, j)),
                  _layer_spec((1, d), layer, lambda i, j: (0, 0)),
                  _layer_spec((d, tn), layer, lambda i, j: (0, j)),
                  _layer_spec((tm, pd), layer, lambda i, j: (i, 0)),
                  _layer_spec((pd, tn), layer, lambda i, j: (0, j))],
        out_specs=pl.BlockSpec((tm, tn), lambda i, j: (i, j)),
        out_shape=jax.ShapeDtypeStruct((m, d), F32),
        scratch_shapes=[pltpu.VMEM((tm, d), _act_dtype(tm))],
        compiler_params=_cparams("parallel", "arbitrary"),
        name="ple",
    )(x, x, _gain(g), wg, p, wp)


def _final_norm_kernel(x_ref, g_ref, o_ref):
    o_ref[...] = _rmsnorm(x_ref[...], g_ref[...])


def _final_norm(x, g, *, tm=512):
    m, d = x.shape
    tm = _tile(m, tm)
    return pl.pallas_call(
        _final_norm_kernel,
        grid=(m // tm,),
        in_specs=[pl.BlockSpec((tm, d), lambda i: (i, 0)),
                  pl.BlockSpec((1, d), lambda i: (0, 0))],
        out_specs=pl.BlockSpec((tm, d), lambda i: (i, 0)),
        out_shape=jax.ShapeDtypeStruct((m, d), F32),
        compiler_params=_cparams("parallel"),
        name="final_norm",
    )(x, g.reshape(1, d))


def _pool_kernel(u_ref, halo_ref, w_ref, s_ref, o_ref, ext_ref, *, tt, start, gw):
    t = pl.program_id(1)

    @pl.when(t == 0)
    def _():
        ext_ref[0:POOL_HALO, :] = halo_ref[0]

    @pl.when(t > 0)
    def _():
        ext_ref[0:POOL_HALO, :] = ext_ref[tt:tt + POOL_HALO, :]

    ext_ref[POOL_HALO:POOL_HALO + tt, :] = u_ref[0]
    pos = start + t * tt + lax.broadcasted_iota(jnp.int32, (tt, 1), 0)
    for g, w in enumerate(POOL_WINDOWS):
        c0, c1 = g * gw, (g + 1) * gw
        cur = ext_ref[POOL_HALO:POOL_HALO + tt, c0:c1]
        acc = cur
        for k in range(1, w):
            acc = acc + ext_ref[POOL_HALO - k:POOL_HALO - k + tt, c0:c1]
        cnt = jnp.minimum(pos + 1, w).astype(F32)
        d = acc / cnt - cur
        y = _wdot(d, w_ref[g])
        o_ref[0, :, c0:c1] = (y * s_ref[:, c0:c1]).astype(o_ref.dtype)


def _pool_mixer(u, buf, start, w_pool, scale, layer, *, tt=512):
    b, t, width = u.shape
    tt = _tile(t, tt)
    gw = width // len(POOL_WINDOWS)
    halo = jnp.pad(buf, ((0, 0), (POOL_HALO - buf.shape[1], 0), (0, 0)))
    return pl.pallas_call(
        functools.partial(_pool_kernel, tt=tt, start=start, gw=gw),
        grid=(b, t // tt),
        in_specs=[pl.BlockSpec((1, tt, width), lambda i, j: (i, j, 0)),
                  pl.BlockSpec((1, POOL_HALO, width), lambda i, j: (i, 0, 0)),
                  _layer_spec(w_pool.shape[1:], layer, lambda i, j: (0, 0, 0)),
                  _layer_spec((1, width), layer, lambda i, j: (0, 0))],
        out_specs=pl.BlockSpec((1, tt, width), lambda i, j: (i, j, 0)),
        out_shape=jax.ShapeDtypeStruct((b, t, width), _act_dtype(tt)),
        scratch_shapes=[pltpu.VMEM((POOL_HALO + max(tt, POOL_HALO), width), F32)],
        compiler_params=_cparams("parallel", "arbitrary"),
        name="pool_mixer",
    )(u, halo, w_pool, _gain(scale))


def _moba_prompt_kernel(q_ref, k_ref, v_ref, o_ref, kmean_ref, kaug_ref, kaug_t_ref, vaug_ref, aug_ref,
                        *, nb, nbp, hd, heads):
    qi = pl.program_id(2)
    blk = MOBA_BLOCK
    group = kmean_ref.shape[0]
    lane = lax.broadcasted_iota(jnp.int32, (blk, LANES), 1)

    @pl.when(qi == 0)
    def _():
        kmean_ref[...] = jnp.zeros_like(kmean_ref)
        offs = lax.broadcasted_iota(jnp.int32, (blk, LANES), 0).astype(F32)
        ones_col = jnp.where(lane == 0, 1.0, 0.0).astype(BF16)
        for g in range(group):
            cols = slice(g * hd, (g + 1) * hd)
            for n in range(nb):
                rows = slice(n * blk, (n + 1) * blk)
                kb = k_ref[0, rows, cols]
                kmean_ref[g, n:n + 1, :] = jnp.sum(kb, axis=0, keepdims=True) / blk
                ext = jnp.where(lane == n, 1.0, 0.0)
                ext = jnp.where((lane >= nbp) & (lane < nbp + 3), float(n * blk), ext)
                ext = jnp.where((lane >= nbp + 3) & (lane < nbp + 6), offs, ext)
                kaug_ref[g, rows, 0:hd] = kb.astype(BF16)
                kaug_ref[g, rows, hd:] = ext.astype(BF16)
                kaug_t_ref[g, 0:hd, rows] = kb.T.astype(BF16)
                kaug_t_ref[g, hd:, rows] = ext.T.astype(BF16)
                vaug_ref[g, rows, 0:hd] = v_ref[0, rows, cols].astype(BF16)
                vaug_ref[g, rows, hd:] = ones_col

    bid = lax.broadcasted_iota(jnp.int32, (nbp, blk), 0)
    bid_f = bid.astype(F32)
    fully_past = bid < qi
    part = lax.broadcasted_iota(jnp.int32, (SUBLANES, blk), 0)
    own = pl.ds(pl.multiple_of(qi * blk, blk), blk)
    row = lax.broadcasted_iota(jnp.int32, (blk, blk), 0)
    col = lax.broadcasted_iota(jnp.int32, (blk, blk), 1)

    def prepare(g):
        h = pl.program_id(1) * group + g
        slope = jnp.exp2(-jnp.full((1, 1), h + 1, jnp.int32).astype(F32) * (8.0 / heads))
        q = q_ref[0, :, g * hd:(g + 1) * hd]

        gate = jnp.where(fully_past, _dot_nt_3pass(kmean_ref[g], q), NEG_INF)
        sel = jnp.zeros((nbp, blk), jnp.bool_)
        for _ in range(MOBA_TOPK):
            m = jnp.max(gate, axis=0, keepdims=True)
            first = jnp.min(jnp.where(gate == m, bid_f, float(nbp)), axis=0, keepdims=True)
            pick = bid_f == first
            sel = sel | pick
            gate = jnp.where(pick, NEG_INF, gate)

        s_hi, s_mid, s_lo = (p.astype(F32) for p in _split3(slope))
        slope_rows = jnp.where(part % 3 == 0, s_hi, jnp.where(part % 3 == 1, s_mid, s_lo))
        slope_rows = jnp.where(part < 6, slope_rows, 0.0)
        aug_ref[g] = jnp.zeros(aug_ref.shape[1:], F32)
        aug_ref[g, 0:nbp, :] = jnp.where(sel & fully_past, 0.0, MASKED)
        aug_ref[g, nbp:nbp + SUBLANES, :] = slope_rows
        q_ext = aug_ref[g].T
        qs = (q * hd ** -0.5).astype(BF16)
        q_aug = jnp.concatenate([qs, q_ext.astype(BF16)], axis=1)
        q_own = jnp.concatenate([qs, jnp.where(lane < nbp, 0.0, q_ext).astype(BF16)], axis=1)
        s_own = jnp.where(col <= row, _dot_nt(q_own, kaug_ref[g, own, :]), MASKED)
        return q_aug, s_own, vaug_ref[g, own, :]

    prepared = [prepare(g) for g in range(group)]

    def finish(n_past):
        scores = [[_dot(q_aug, kaug_t_ref[g, :, n * blk:(n + 1) * blk]) for n in range(n_past)]
                  for g, (q_aug, _, _) in enumerate(prepared)]
        for g, (_, s_own, v_own) in enumerate(prepared):
            m_run = s_own
            for s_n in scores[g]:
                m_run = jnp.maximum(m_run, s_n)
            m = jnp.max(m_run, axis=-1, keepdims=True)
            acc = _dot(jnp.exp(s_own - m).astype(BF16), v_own)
            for n, s_n in enumerate(scores[g]):
                acc = acc + _dot(jnp.exp(s_n - m).astype(BF16), vaug_ref[g, n * blk:(n + 1) * blk, :])
            o_ref[0, :, g * hd:(g + 1) * hd] = (acc[:, 0:hd] / acc[:, hd:hd + 1]).astype(o_ref.dtype)

    @pl.when(qi == 0)
    def _():
        finish(0)

    lo = 0
    while lo < nb - 1:
        hi = min(lo + MOBA_VARIANT_BLOCKS, nb - 1)

        @pl.when((qi > lo) & (qi <= hi))
        def _(hi=hi):
            finish(hi)

        lo = hi


def _moba_prompt(q, k, v):
    b, t, width = q.shape
    blk, hd, group = MOBA_BLOCK, LANES, MOBA_HEAD_GROUP
    heads = width // hd
    assert t % blk == 0 and width % hd == 0 and heads % group == 0
    nb = t // blk
    nbp = -(-nb // SUBLANES) * SUBLANES
    assert nb >= MOBA_TOPK and nbp + SUBLANES <= LANES
    return pl.pallas_call(
        functools.partial(_moba_prompt_kernel, nb=nb, nbp=nbp, hd=hd, heads=heads),
        grid=(b, heads // group, nb),
        in_specs=[pl.BlockSpec((1, blk, group * hd), lambda i, h, j: (i, j, h)),
                  pl.BlockSpec((1, t, group * hd), lambda i, h, j: (i, 0, h)),
                  pl.BlockSpec((1, t, group * hd), lambda i, h, j: (i, 0, h))],
        out_specs=pl.BlockSpec((1, blk, group * hd), lambda i, h, j: (i, j, h)),
        out_shape=jax.ShapeDtypeStruct((b, t, width), BF16),
        scratch_shapes=[pltpu.VMEM((group, nbp, hd), F32),
                        pltpu.VMEM((group, t, 2 * hd), BF16),
                        pltpu.VMEM((group, 2 * hd, t), BF16),
                        pltpu.VMEM((group, t, 2 * hd), BF16),
                        pltpu.VMEM((group, LANES, blk), F32)],
        compiler_params=_cparams("parallel", "parallel", "arbitrary"),
        name="moba_prompt",
    )(q, k, v)


PAGES_PER_STEP = 16


def _block_sum_kernel(pt_ref, *refs, page, blk):
    del pt_ref
    k_refs, o_ref = refs[:-1], refs[-1]
    per_blk = blk // page
    for n in range(len(k_refs) // per_blk):
        acc = jnp.sum(k_refs[n * per_blk][...], axis=0)
        for r in range(1, per_blk):
            acc = acc + jnp.sum(k_refs[n * per_blk + r][...], axis=0)
        o_ref[0, n] = acc


def _paged_block_sums(cache, layer, page_table):
    _, _, page, heads, hd = cache.shape
    b, n_pages = page_table.shape
    pps = PAGES_PER_STEP
    assert MOBA_BLOCK % page == 0 and (pps * page) % MOBA_BLOCK == 0 and n_pages % pps == 0
    blocks_per_step = pps * page // MOBA_BLOCK

    def page_spec(r):
        return pl.BlockSpec((None, None, page, heads, hd),
                            lambda i, s, pt: (layer, pt[i, s * pps + r], 0, 0, 0))

    return pl.pallas_call(
        functools.partial(_block_sum_kernel, page=page, blk=MOBA_BLOCK),
        grid_spec=pltpu.PrefetchScalarGridSpec(
            num_scalar_prefetch=1,
            grid=(b, n_pages // pps),
            in_specs=[page_spec(r) for r in range(pps)],
            out_specs=pl.BlockSpec((1, blocks_per_step, heads, hd), lambda i, s, pt: (i, s, 0, 0)),
        ),
        out_shape=jax.ShapeDtypeStruct((b, n_pages * page // MOBA_BLOCK, heads, hd), F32),
        compiler_params=_cparams("parallel", "arbitrary"),
        name="paged_block_sums",
    )(page_table, *([cache] * pps))


def _decode_select_kernel(q_ref, ks_ref, o_ref, *, nb):
    gate = jnp.sum(ks_ref[0] * q_ref[0][None], axis=-1, keepdims=True) / MOBA_BLOCK
    idx = lax.broadcasted_iota(jnp.int32, gate.shape, 0)
    for r in range(MOBA_TOPK):
        m = jnp.max(gate, axis=0, keepdims=True)
        first = jnp.min(jnp.where(gate == m, idx, nb), axis=0, keepdims=True)
        o_ref[0, r] = jnp.broadcast_to(first[0], o_ref.shape[2:])
        gate = jnp.where(idx == first, NEG_INF, gate)


def _decode_select(q, ksum):
    b, nb, heads, hd = ksum.shape
    assert nb >= MOBA_TOPK
    out = pl.pallas_call(
        functools.partial(_decode_select_kernel, nb=nb),
        grid=(b,),
        in_specs=[pl.BlockSpec((1, heads, hd), lambda i: (i, 0, 0)),
                  pl.BlockSpec((1, nb, heads, hd), lambda i: (i, 0, 0, 0))],
        out_specs=pl.BlockSpec((1, MOBA_TOPK, heads, LANES), lambda i: (i, 0, 0, 0)),
        out_shape=jax.ShapeDtypeStruct((b, MOBA_TOPK, heads, LANES), jnp.int32),
        compiler_params=_cparams("parallel"),
        name="decode_select",
    )(q, ksum)
    return out[..., 0]


def _decode_attn_kernel(pt_ref, sel_ref, q_ref, kn_ref, vn_ref, *refs,
                        page, heads, past_len, per_blk):
    del pt_ref
    n_parts = MOBA_TOPK * per_blk
    k_refs, v_refs, o_ref = refs[:n_parts], refs[n_parts:2 * n_parts], refs[2 * n_parts]
    i, h = pl.program_id(0), pl.program_id(1)
    hd = q_ref.shape[2]
    q = q_ref[0, pl.ds(h, 1), :]
    scale = hd ** -0.5
    slope = jnp.exp2(-jnp.full((1, 1), h + 1, jnp.int32).astype(F32) * (8.0 / heads))

    m = jnp.sum(q * kn_ref[0, pl.ds(h, 1), :], axis=-1, keepdims=True) * scale
    l = jnp.ones_like(m)
    acc = vn_ref[0, pl.ds(h, 1), :]
    r = lax.broadcasted_iota(jnp.int32, (page, 1), 0)
    for j in range(MOBA_TOPK):
        n = sel_ref[(i * MOBA_TOPK + j) * heads + h]
        for part in range(per_blk):
            k_h = k_refs[j * per_blk + part][pl.ds(h, page, stride=heads), :]
            v_h = v_refs[j * per_blk + part][pl.ds(h, page, stride=heads), :]
            kpos = n * MOBA_BLOCK + part * page + r
            s = jnp.sum(k_h * q, axis=-1, keepdims=True) * scale - slope * (past_len - kpos).astype(F32)
            m_new = jnp.maximum(m, jnp.max(s, axis=0, keepdims=True))
            alpha = jnp.exp(m - m_new)
            p = jnp.exp(s - m_new)
            l = alpha * l + jnp.sum(p, axis=0, keepdims=True)
            acc = alpha * acc + jnp.sum(p * v_h, axis=0, keepdims=True)
            m = m_new
    o_ref[0, pl.ds(h, 1), :] = acc / l


def _moba_decode(q, k_new, v_new, cache_k, cache_v, layer, page_table, sel):
    b, heads, hd = q.shape
    n_layers, n_pool, page = cache_k.shape[:3]
    per_blk = MOBA_BLOCK // page
    past_len = page_table.shape[1] * page
    assert past_len % MOBA_BLOCK == 0
    rows = page * heads
    cache_k = cache_k.reshape(n_layers, n_pool, rows, hd)
    cache_v = cache_v.reshape(n_layers, n_pool, rows, hd)

    def page_spec(j, part):
        def index(i, h, pt, sl):
            n = sl[(i * MOBA_TOPK + j) * heads + h]
            return (layer, pt[i, n * per_blk + part], 0, 0)
        return pl.BlockSpec((None, None, rows, hd), index)

    tok_spec = pl.BlockSpec((1, heads, hd), lambda i, h, pt, sl: (i, 0, 0))
    page_specs = [page_spec(j, r) for j in range(MOBA_TOPK) for r in range(per_blk)]
    return pl.pallas_call(
        functools.partial(_decode_attn_kernel, page=page, heads=heads, past_len=past_len, per_blk=per_blk),
        grid_spec=pltpu.PrefetchScalarGridSpec(
            num_scalar_prefetch=2,
            grid=(b, heads),
            in_specs=[tok_spec, tok_spec, tok_spec] + page_specs * 2,
            out_specs=tok_spec,
        ),
        out_shape=jax.ShapeDtypeStruct((b, heads, hd), F32),
        compiler_params=_cparams("parallel", "arbitrary"),
        name="moba_decode",
    )(page_table, sel.reshape(-1), q, k_new, v_new,
      *([cache_k] * len(page_specs)), *([cache_v] * len(page_specs)))


def _gla_gate_kernel(x_ref, g_ref, wa_ref, w2_ref, b_ref, o_ref):
    hn = _rmsnorm(x_ref[...], g_ref[...])
    a = _wdot(hn, wa_ref[...], w_rows=True)
    z = _wdot(a, w2_ref[...]) + b_ref[...]
    o_ref[...] = jax.nn.log_sigmoid(z) / GLA_GATE_TAU


def _gla_gate(x, g, wa, w2, bias, layer, wlayer, *, tm=512):
    m, d = x.shape
    rank, n = w2.shape[1:]
    tm = _tile(m, tm)
    return pl.pallas_call(
        _gla_gate_kernel,
        grid=(m // tm,),
        in_specs=[pl.BlockSpec((tm, d), lambda i: (i, 0)),
                  _layer_spec((1, d), layer, lambda i: (0, 0)),
                  _layer_spec((rank, d), wlayer, lambda i: (0, 0)),
                  _layer_spec((rank, n), wlayer, lambda i: (0, 0)),
                  _layer_spec((1, n), wlayer, lambda i: (0, 0))],
        out_specs=pl.BlockSpec((tm, n), lambda i: (i, 0)),
        out_shape=jax.ShapeDtypeStruct((m, n), F32),
        compiler_params=_cparams("parallel"),
        name="gla_gate",
    )(x, _gain(g), wa, w2, _gain(bias))


def _gla_prompt_kernel(z_ref, lg_ref, gg_ref, o_ref, s_ref, *, heads, dk, dv):
    c = GLA_CHUNK

    @pl.when(pl.program_id(0) == 0)
    def _():
        s_ref[...] = jnp.zeros_like(s_ref)

    row = lax.broadcasted_iota(jnp.int32, (c, c), 0)
    col = lax.broadcasted_iota(jnp.int32, (c, c), 1)
    causal = col <= row
    tri = jnp.where(causal, 1.0, 0.0).astype(BF16)
    k0, v0, r0 = heads * dk, 2 * heads * dk, 2 * heads * dk + heads * dv
    for bi in range(z_ref.shape[0]):
        for h in range(heads):
            q = z_ref[bi, :, h * dk:(h + 1) * dk] * dk ** -0.5
            k = z_ref[bi, :, k0 + h * dk:k0 + (h + 1) * dk]
            v = z_ref[bi, :, v0 + h * dv:v0 + (h + 1) * dv].astype(BF16)
            lg_hi, lg_mid, lg_lo = _split3(lg_ref[bi, :, h * dk:(h + 1) * dk])
            b = _dot(tri, lg_hi) + (_dot(tri, lg_mid) + _dot(tri, lg_lo))
            b_last = b[c - 1:c, :]
            q_dec = (q * jnp.exp(b)).astype(BF16)
            a = _dot_nt(q_dec, (k * jnp.exp(-b)).astype(BF16))
            a = jnp.where(causal, a, 0.0)
            state = s_ref[bi, h]
            o = _dot(a.astype(BF16), v) + _dot(q_dec, state.astype(BF16))
            k_rem = (k * jnp.exp(b_last - b)).astype(BF16)
            decay = jnp.exp(jnp.broadcast_to(b_last, (LANES, dk))).T[:, 0:1]
            s_ref[bi, h] = decay * state + _dot_tn(k_rem, v)
            r = z_ref[bi, :, r0 + h * dv:r0 + (h + 1) * dv]
            o_ref[bi, :, h * dv:(h + 1) * dv] = (
                _rmsnorm(o, gg_ref[...]) * (r * jax.nn.sigmoid(r))).astype(o_ref.dtype)


def _gla_prompt(z, lg, g_gla, layer, *, heads, dk, dv):
    b, t, zw = z.shape
    c = GLA_CHUNK
    assert t % c == 0 and zw == 2 * heads * (dk + dv)
    return pl.pallas_call(
        functools.partial(_gla_prompt_kernel, heads=heads, dk=dk, dv=dv),
        grid=(t // c,),
        in_specs=[pl.BlockSpec((b, c, zw), lambda j: (0, j, 0)),
                  pl.BlockSpec((b, c, heads * dk), lambda j: (0, j, 0)),
                  _layer_spec((1, dv), layer, lambda j: (0, 0))],
        out_specs=[pl.BlockSpec((b, c, heads * dv), lambda j: (0, j, 0)),
                   pl.BlockSpec((b, heads, dk, dv), lambda j: (0, 0, 0, 0))],
        out_shape=[jax.ShapeDtypeStruct((b, t, heads * dv), BF16),
                   jax.ShapeDtypeStruct((b, heads, dk, dv), F32)],
        compiler_params=_cparams("arbitrary"),
        name="gla_prompt",
    )(z, lg, _gain(g_gla))


def _gla_decode_kernel(q_ref, k_ref, lg_ref, v_ref, r_ref, gg_ref, s0_ref, o_ref, s_ref, *, dk):
    q = q_ref[0, 0] * dk ** -0.5
    k = k_ref[0, 0]
    b = lg_ref[0, 0]
    v = v_ref[0, 0]
    state = s0_ref[0, 0]
    q_dec = q * jnp.exp(b)
    a = jnp.sum(q_dec * (k * jnp.exp(-b)), axis=0, keepdims=True)
    o = a * v + jnp.sum(q_dec * state, axis=0, keepdims=True)
    s_ref[0, 0] = jnp.exp(b) * state + (k * jnp.exp(b - b)) * v
    r = r_ref[0, 0]
    o_ref[0, 0] = _rmsnorm(o, gg_ref[...]) * (r * jax.nn.sigmoid(r))


def _gla_decode(q, k, lg, v, r, g_gla, s0, layer):
    _, b, heads, dk, dv = s0.shape
    col = pl.BlockSpec((1, 1, dk, 1), lambda i, h: (i, h, 0, 0))
    rowspec = pl.BlockSpec((1, 1, 1, dv), lambda i, h: (i, h, 0, 0))
    st = pl.BlockSpec((1, 1, dk, dv), lambda i, h: (i, h, 0, 0))
    return pl.pallas_call(
        functools.partial(_gla_decode_kernel, dk=dk),
        grid=(b, heads),
        in_specs=[col, col, col, rowspec, rowspec,
                  _layer_spec((1, dv), layer, lambda i, h: (0, 0)),
                  _layer_spec((1, 1, dk, dv), layer, lambda i, h: (i, h, 0, 0))],
        out_specs=[rowspec, st],
        out_shape=[jax.ShapeDtypeStruct((b, heads, 1, dv), F32),
                   jax.ShapeDtypeStruct((b, heads, dk, dv), s0.dtype)],
        compiler_params=_cparams("parallel", "parallel"),
        name="gla_decode",
    )(q, k, lg, v, r, _gain(g_gla), s0)


def _trunk(x, p, start, pool_state, gla_state, cache_k, cache_v, page_table, wts):
    b, t, d = x.shape
    m = b * t
    depth = wts["g_mix"].shape[0]
    pool_w = wts["pool_scale"].shape[1]
    pool_buf = max(POOL_WINDOWS) - 1
    heads_m, hd = cache_k.shape[3], cache_k.shape[4]
    moba_w = heads_m * hd
    _, _, heads_g, dk, dv = gla_state.shape
    dk_tot, dv_tot = heads_g * dk, heads_g * dv
    decode = page_table is not None
    assert pool_w == moba_w
    p = p.reshape(depth, m, -1)

    h = x.reshape(m, d)
    pools, ks, vs, glas = [], [], [], []
    for i in range(depth):
        j = i // 2
        if i % 2 == 0:
            u, q, k, v = (a.reshape(b, t, -1) for a in _norm_matmul(
                h, wts["g_mix"], wts["w_in_even"], i, j, pool_w + 3 * moba_w, n_out=4, tm=512, tn=1024))
            if decode:
                buf = pool_state[j]
                pools.append(jnp.concatenate([buf, u], axis=1)[:, -pool_buf:])
            else:
                buf = jnp.zeros((b, pool_buf, pool_w), x.dtype)
                pools.append(u[:, -pool_buf:])
            y_a = _pool_mixer(u, buf, start, wts["w_pool"], wts["pool_scale"], j)
            if decode:
                q3, k3, v3 = (a.reshape(b, heads_m, hd) for a in (q, k, v))
                ksum = _paged_block_sums(cache_k, j, page_table)
                sel = _decode_select(q3, ksum)
                y_b = _moba_decode(q3, k3, v3, cache_k, cache_v, j, page_table, sel)
            else:
                y_b = _moba_prompt(q, k, v)
            h = _proj_res([y_a.reshape(m, pool_w), y_b.reshape(m, moba_w)], wts["w_out_even"], j, h)
            ks.append(k.reshape(b, t, heads_m, hd))
            vs.append(v.reshape(b, t, heads_m, hd))
        else:
            n_main = 2 * dk_tot + 2 * dv_tot
            z = _norm_matmul(h, wts["g_mix"], wts["w_in_odd"], i, j, n_main, w_rows=True)
            lg = _gla_gate(h, wts["g_mix"], wts["w_gate_odd"], wts["w_gk2"], wts["b_gk"], i, j)
            if decode:
                def cols(a):
                    return a.reshape(b, heads_g, dk, 1)
                def rows(a):
                    return a.reshape(b, heads_g, 1, dv)
                og, s_new = _gla_decode(
                    cols(z[:, :dk_tot]), cols(z[:, dk_tot:2 * dk_tot]), cols(lg),
                    rows(z[:, 2 * dk_tot:2 * dk_tot + dv_tot]), rows(z[:, 2 * dk_tot + dv_tot:]),
                    wts["g_gla"], gla_state, j)
            else:
                og, s_new = _gla_prompt(z.reshape(b, t, n_main), lg.reshape(b, t, dk_tot), wts["g_gla"], j,
                                        heads=heads_g, dk=dk, dv=dv)
            h = _proj_res([og.reshape(m, dv_tot)], wts["w_out_odd"], j, h)
            glas.append(s_new.astype(gla_state.dtype))
        h = _ffn(h, wts["g_ffn"], wts["w_ffn_gate"], wts["w_ffn_up"], wts["w_ffn_down"], i)
        h = _ple(h, wts["g_ple"], wts["w_ple_gate"], p, wts["w_ple_proj"], i)
    y = _final_norm(h, wts["g_final"]).reshape(b, t, d)
    return y, jnp.stack(pools), jnp.stack(ks), jnp.stack(vs), jnp.stack(glas)


_MATMUL_WEIGHTS = ("w_in_even", "w_pool", "w_out_even", "w_in_odd", "w_gk2", "w_out_odd",
                   "w_ffn_gate", "w_ffn_up", "w_ffn_down", "w_ple_gate", "w_ple_proj")


def kernel(x_prompt, x_sample, state_pool, cache_k, cache_v, state_gla, page_table, p_prompt, p_sample,
           g_mix, g_ffn, g_ple, g_final, w_in_even, w_pool, pool_scale, w_out_even,
           w_in_odd, w_gk2, b_gk, g_gla, w_out_odd, w_ffn_gate, w_ffn_up, w_ffn_down,
           w_ple_gate, w_ple_proj):
    wts = dict(g_mix=g_mix, g_ffn=g_ffn, g_ple=g_ple, g_final=g_final, w_in_even=w_in_even,
               w_pool=w_pool, pool_scale=pool_scale, w_out_even=w_out_even, w_in_odd=w_in_odd,
               w_gk2=w_gk2, b_gk=b_gk, g_gla=g_gla, w_out_odd=w_out_odd, w_ffn_gate=w_ffn_gate,
               w_ffn_up=w_ffn_up, w_ffn_down=w_ffn_down, w_ple_gate=w_ple_gate, w_ple_proj=w_ple_proj)
    n_main = 2 * state_gla.shape[2] * (state_gla.shape[3] + state_gla.shape[4])
    wts["w_in_odd"] = jnp.swapaxes(w_in_odd, 1, 2)
    wts["w_gate_odd"] = wts["w_in_odd"][:, n_main:, :]
    wts_bf16 = dict(wts)
    for name in _MATMUL_WEIGHTS + ("w_gate_odd",):
        wts_bf16[name] = wts[name].astype(BF16)
    past_len = page_table.shape[1] * cache_k.shape[2]
    gla0 = jax.ShapeDtypeStruct((state_gla.shape[0], x_prompt.shape[0]) + state_gla.shape[2:], state_gla.dtype)
    y_p, pool_p, k_p, v_p, gla_p = _trunk(x_prompt, p_prompt, 0, None, gla0, cache_k, cache_v, None, wts_bf16)
    y_s, pool_s, k_s, v_s, gla_s = _trunk(x_sample, p_sample, past_len, state_pool, state_gla,
                                          cache_k, cache_v, page_table, wts)
    return (y_p, y_s, pool_p, pool_s, k_p, k_s, v_p, v_s, gla_p, gla_s)
```

```python
import functools

import jax
import jax.numpy as jnp
from jax import lax
from jax.experimental import pallas as pl
from jax.experimental.pallas import tpu as pltpu

F32 = jnp.float32
BF16 = jnp.bfloat16

RMS_EPS = 1e-6
POOL_WINDOWS = (2, 4, 8, 16)
POOL_HALO = 16
MOBA_BLOCK = 256
MOBA_TOPK = 3
MOBA_VARIANT_BLOCKS = 4
MOBA_HEAD_GROUP = 2
GLA_GATE_TAU = 16.0
GLA_CHUNK = 64
LANES = 128
SUBLANES = 8
VMEM_LIMIT = 52 * 1024 * 1024
NEG_INF = float("-inf")
MASKED = -1e30


def _cparams(*sem):
    return pltpu.CompilerParams(dimension_semantics=sem, vmem_limit_bytes=VMEM_LIMIT)


def _rmsnorm(x, g):
    ms = jnp.mean(x * x, axis=-1, keepdims=True)
    return x * lax.rsqrt(ms + RMS_EPS) * g


def _dot(a, b):
    return jnp.dot(a, b, preferred_element_type=F32)


def _wdot(a, w, w_rows=False):
    dims = (((1,), (1 if w_rows else 0,)), ((), ()))
    if w.dtype == F32:
        return lax.dot_general(a.astype(F32), w, dims, precision=lax.Precision.HIGHEST,
                               preferred_element_type=F32)
    return lax.dot_general(a.astype(BF16), w, dims, preferred_element_type=F32)


def _dot_nt(a, b):
    return lax.dot_general(a, b, (((1,), (1,)), ((), ())), preferred_element_type=F32)


def _dot_tn(a, b):
    return lax.dot_general(a, b, (((0,), (0,)), ((), ())), preferred_element_type=F32)


def _split2(a):
    hi = a.astype(BF16)
    lo = (a - hi.astype(F32)).astype(BF16)
    return hi, lo


def _split3(a):
    hi = a.astype(BF16)
    r = a - hi.astype(F32)
    mid = r.astype(BF16)
    lo = (r - mid.astype(F32)).astype(BF16)
    return hi, mid, lo


def _dot_nt_3pass(a, b):
    ah, al = _split2(a)
    bh, bl = _split2(b)
    return _dot_nt(ah, bh) + (_dot_nt(ah, bl) + _dot_nt(al, bh))


def _tile(m, t):
    t = min(m, t)
    assert m % t == 0, (m, t)
    return t


def _act_dtype(rows):
    return BF16 if rows % 16 == 0 else F32


def _layer_spec(tail, layer, index):
    return pl.BlockSpec((None,) + tuple(tail), lambda *a: (layer,) + tuple(index(*a)))


def _gain(g):
    return g.reshape(g.shape[0], 1, g.shape[1])


def _norm_matmul_kernel(x_ref, g_ref, w_ref, *refs, steps_per_out, w_rows):
    o_refs, hn_ref = refs[:-1], refs[-1]
    j = pl.program_id(1)

    @pl.when(j == 0)
    def _():
        hn_ref[...] = _rmsnorm(x_ref[...], g_ref[...]).astype(hn_ref.dtype)

    res = _wdot(hn_ref[...], w_ref[...], w_rows)
    if len(o_refs) == 1:
        o_refs[0][...] = res
    else:
        for idx, o_ref in enumerate(o_refs):
            @pl.when(j // steps_per_out == idx)
            def _(o_ref=o_ref):
                o_ref[...] = res


def _norm_matmul(x, g, w, layer, wlayer, n, *, n_out=1, w_rows=False, tm=1024, tn=512):
    m, d = x.shape
    group = n // n_out
    tm, tn = _tile(m, tm), _tile(group, tn)
    spo = group // tn

    def out_index(idx):
        return lambda i, j: (i, jnp.clip(j - idx * spo, 0, spo - 1))

    outs = pl.pallas_call(
        functools.partial(_norm_matmul_kernel, steps_per_out=spo, w_rows=w_rows),
        grid=(m // tm, n // tn),
        in_specs=[pl.BlockSpec((tm, d), lambda i, j: (i, 0)),
                  _layer_spec((1, d), layer, lambda i, j: (0, 0)),
                  (_layer_spec((tn, d), wlayer, lambda i, j: (j, 0)) if w_rows else
                   _layer_spec((d, tn), wlayer, lambda i, j: (0, j)))],
        out_specs=[pl.BlockSpec((tm, tn), out_index(idx)) for idx in range(n_out)],
        out_shape=[jax.ShapeDtypeStruct((m, group), F32)] * n_out,
        scratch_shapes=[pltpu.VMEM((tm, d), _act_dtype(tm))],
        compiler_params=_cparams("parallel", "arbitrary"),
        name="norm_matmul",
    )(x, _gain(g), w)
    return outs[0] if n_out == 1 else outs


def _proj_res_kernel(*refs, n_in):
    a_refs, w_refs = refs[:n_in], refs[n_in:2 * n_in]
    res_ref, o_ref = refs[2 * n_in], refs[2 * n_in + 1]
    acc = res_ref[...]
    for a_ref, w_ref in zip(a_refs, w_refs):
        acc = acc + _wdot(a_ref[...], w_ref[...])
    o_ref[...] = acc


def _proj_res(a_list, w, layer, res, *, tm=1024, tn=512):
    m, n = res.shape
    tm, tn = _tile(m, tm), _tile(n, tn)
    n_in = len(a_list)
    kw = a_list[0].shape[1]
    assert all(a.shape[1] == kw for a in a_list) and kw * n_in == w.shape[1]
    in_specs = ([pl.BlockSpec((tm, kw), lambda i, j: (i, 0)) for _ in a_list]
                + [_layer_spec((kw, tn), layer, lambda i, j, r=r: (r, j)) for r in range(n_in)]
                + [pl.BlockSpec((tm, tn), lambda i, j: (i, j))])
    return pl.pallas_call(
        functools.partial(_proj_res_kernel, n_in=n_in),
        grid=(m // tm, n // tn),
        in_specs=in_specs,
        out_specs=pl.BlockSpec((tm, tn), lambda i, j: (i, j)),
        out_shape=jax.ShapeDtypeStruct((m, n), F32),
        compiler_params=_cparams("parallel", "arbitrary"),
        name="proj_res",
    )(*a_list, *([w] * n_in), res)


def _ffn_kernel(x_ref, g_ref, wg_ref, wu_ref, wd_ref, o_ref, *rest):
    cast_refs, hn_ref = rest[:-1], rest[-1]

    @pl.when(pl.program_id(1) == 0)
    def _():
        x = x_ref[...]
        hn_ref[...] = _rmsnorm(x, g_ref[...]).astype(hn_ref.dtype)
        o_ref[...] = x

    hn = hn_ref[...]
    a = _wdot(hn, wg_ref[...])
    b = _wdot(hn, wu_ref[...])
    h = (a * jax.nn.sigmoid(a)) * b
    o_ref[...] += _wdot(h, wd_ref[...])
    for w_ref, c_ref in zip((wg_ref, wu_ref, wd_ref), cast_refs):
        c_ref[...] = w_ref[...].astype(c_ref.dtype)


def _ffn(x, g, w3, layer, *, emit_bf16=False, tm=512, tf=512):
    m, d = x.shape
    wg, wu, wd = w3
    stacked = wg.ndim == 3
    f = wg.shape[-1]
    tm, tf = _tile(m, tm), _tile(f, tf)
    assert not emit_bf16 or m == tm

    def wspec(tail, index):
        return _layer_spec(tail, layer, index) if stacked else pl.BlockSpec(tail, index)

    up_spec = ((d, tf), lambda i, j: (0, j))
    down_spec = ((tf, d), lambda i, j: (j, 0))
    out_specs = [pl.BlockSpec((tm, d), lambda i, j: (i, 0))]
    out_shape = [jax.ShapeDtypeStruct((m, d), F32)]
    if emit_bf16:
        out_specs += [pl.BlockSpec(*up_spec), pl.BlockSpec(*up_spec), pl.BlockSpec(*down_spec)]
        out_shape += [jax.ShapeDtypeStruct((d, f), BF16), jax.ShapeDtypeStruct((d, f), BF16),
                      jax.ShapeDtypeStruct((f, d), BF16)]
    outs = pl.pallas_call(
        _ffn_kernel,
        grid=(m // tm, f // tf),
        in_specs=[pl.BlockSpec((tm, d), lambda i, j: (i, 0)),
                  _layer_spec((1, d), layer, lambda i, j: (0, 0)),
                  wspec(*up_spec), wspec(*up_spec), wspec(*down_spec)],
        out_specs=out_specs,
        out_shape=out_shape,
        scratch_shapes=[pltpu.VMEM((tm, d), _act_dtype(tm))],
        compiler_params=_cparams("parallel", "arbitrary"),
        name="ffn",
    )(x, _gain(g), wg, wu, wd)
    return (outs[0], tuple(outs[1:])) if emit_bf16 else outs[0]


def _ple_kernel(x_ref, xj_ref, g_ref, wg_ref, p_ref, wp_ref, o_ref, hn_ref):
    @pl.when(pl.program_id(1) == 0)
    def _():
        hn_ref[...] = _rmsnorm(x_ref[...], g_ref[...]).astype(hn_ref.dtype)

    gate = _wdot(hn_ref[...], wg_ref[...])
    proj = _wdot(p_ref[...], wp_ref[...])
    o_ref[...] = xj_ref[...] + jax.nn.sigmoid(gate) * proj


def _ple(x, g, wg, p, wp, layer, *, tm=1024, tn=512):
    m, d = x.shape
    tm, tn = _tile(m, tm), _tile(d, tn)
    pd = p.shape[2]
    return pl.pallas_call(
        _ple_kernel,
        grid=(m // tm, d // tn),
        in_specs=[pl.BlockSpec((tm, d), lambda i, j: (i, 0)),
                  pl.BlockSpec((tm, tn), lambda i, j: (i, j)),
                  _layer_spec((1, d), layer, lambda i, j: (0, 0)),
                  _layer_spec((d, tn), layer, lambda i, j: (0, j)),
                  _layer_spec((tm, pd), layer, lambda i, j: (i, 0)),
                  _layer_spec((pd, tn), layer, lambda i, j: (0, j))],
        out_specs=pl.BlockSpec((tm, tn), lambda i, j: (i, j)),
        out_shape=jax.ShapeDtypeStruct((m, d), F32),
        scratch_shapes=[pltpu.VMEM((tm, d), _act_dtype(tm))],
        compiler_params=_cparams("parallel", "arbitrary"),
        name="ple",
    )(x, x, _gain(g), wg, p, wp)


def _final_norm_kernel(x_ref, g_ref, o_ref):
    o_ref[...] = _rmsnorm(x_ref[...], g_ref[...])


def _final_norm(x, g, *, tm=512):
    m, d = x.shape
    tm = _tile(m, tm)
    return pl.pallas_call(
        _final_norm_kernel,
        grid=(m // tm,),
        in_specs=[pl.BlockSpec((tm, d), lambda i: (i, 0)),
                  pl.BlockSpec((1, d), lambda i: (0, 0))],
        out_specs=pl.BlockSpec((tm, d), lambda i: (i, 0)),
        out_shape=jax.ShapeDtypeStruct((m, d), F32),
        compiler_params=_cparams("parallel"),
        name="final_norm",
    )(x, g.reshape(1, d))


def _pool_kernel(u_ref, halo_ref, w_ref, s_ref, o_ref, ext_ref, *, tt, start, gw):
    t = pl.program_id(1)

    @pl.when(t == 0)
    def _():
        ext_ref[0:POOL_HALO, :] = halo_ref[0]

    @pl.when(t > 0)
    def _():
        ext_ref[0:POOL_HALO, :] = ext_ref[tt:tt + POOL_HALO, :]

    ext_ref[POOL_HALO:POOL_HALO + tt, :] = u_ref[0]
    pos = start + t * tt + lax.broadcasted_iota(jnp.int32, (tt, 1), 0)
    for g, w in enumerate(POOL_WINDOWS):
        c0, c1 = g * gw, (g + 1) * gw
        cur = ext_ref[POOL_HALO:POOL_HALO + tt, c0:c1]
        acc = cur
        for k in range(1, w):
            acc = acc + ext_ref[POOL_HALO - k:POOL_HALO - k + tt, c0:c1]
        cnt = jnp.minimum(pos + 1, w).astype(F32)
        d = acc / cnt - cur
        y = _wdot(d, w_ref[g])
        o_ref[0, :, c0:c1] = (y * s_ref[:, c0:c1]).astype(o_ref.dtype)


def _pool_mixer(u, buf, start, w_pool, scale, layer, *, tt=512):
    b, t, width = u.shape
    tt = _tile(t, tt)
    gw = width // len(POOL_WINDOWS)
    halo = jnp.pad(buf, ((0, 0), (POOL_HALO - buf.shape[1], 0), (0, 0)))
    return pl.pallas_call(
        functools.partial(_pool_kernel, tt=tt, start=start, gw=gw),
        grid=(b, t // tt),
        in_specs=[pl.BlockSpec((1, tt, width), lambda i, j: (i, j, 0)),
                  pl.BlockSpec((1, POOL_HALO, width), lambda i, j: (i, 0, 0)),
                  _layer_spec(w_pool.shape[1:], layer, lambda i, j: (0, 0, 0)),
                  _layer_spec((1, width), layer, lambda i, j: (0, 0))],
        out_specs=pl.BlockSpec((1, tt, width), lambda i, j: (i, j, 0)),
        out_shape=jax.ShapeDtypeStruct((b, t, width), _act_dtype(tt)),
        scratch_shapes=[pltpu.VMEM((POOL_HALO + max(tt, POOL_HALO), width), F32)],
        compiler_params=_cparams("parallel", "arbitrary"),
        name="pool_mixer",
    )(u, halo, w_pool, _gain(scale))


def _moba_prompt_kernel(q_ref, k_ref, v_ref, o_ref, kmean_ref, kaug_ref, kaug_t_ref, vaug_ref, aug_ref,
                        *, nb, nbp, hd, heads):
    qi = pl.program_id(2)
    blk = MOBA_BLOCK
    group = kmean_ref.shape[0]
    lane = lax.broadcasted_iota(jnp.int32, (blk, LANES), 1)

    @pl.when(qi == 0)
    def _():
        kmean_ref[...] = jnp.zeros_like(kmean_ref)
        offs = lax.broadcasted_iota(jnp.int32, (blk, LANES), 0).astype(F32)
        ones_col = jnp.where(lane == 0, 1.0, 0.0).astype(BF16)
        for g in range(group):
            cols = slice(g * hd, (g + 1) * hd)
            for n in range(nb):
                rows = slice(n * blk, (n + 1) * blk)
                kb = k_ref[0, rows, cols]
                kmean_ref[g, n:n + 1, :] = jnp.sum(kb, axis=0, keepdims=True) / blk
                ext = jnp.where(lane == n, 1.0, 0.0)
                ext = jnp.where((lane >= nbp) & (lane < nbp + 3), float(n * blk), ext)
                ext = jnp.where((lane >= nbp + 3) & (lane < nbp + 6), offs, ext)
                kaug_ref[g, rows, 0:hd] = kb.astype(BF16)
                kaug_ref[g, rows, hd:] = ext.astype(BF16)
                kaug_t_ref[g, 0:hd, rows] = kb.T.astype(BF16)
                kaug_t_ref[g, hd:, rows] = ext.T.astype(BF16)
                vaug_ref[g, rows, 0:hd] = v_ref[0, rows, cols].astype(BF16)
                vaug_ref[g, rows, hd:] = ones_col

    bid = lax.broadcasted_iota(jnp.int32, (nbp, blk), 0)
    bid_f = bid.astype(F32)
    fully_past = bid < qi
    part = lax.broadcasted_iota(jnp.int32, (SUBLANES, blk), 0)
    own = pl.ds(pl.multiple_of(qi * blk, blk), blk)
    row = lax.broadcasted_iota(jnp.int32, (blk, blk), 0)
    col = lax.broadcasted_iota(jnp.int32, (blk, blk), 1)

    def prepare(g):
        h = pl.program_id(1) * group + g
        slope = jnp.exp2(-jnp.full((1, 1), h + 1, jnp.int32).astype(F32) * (8.0 / heads))
        q = q_ref[0, :, g * hd:(g + 1) * hd]

        gate = jnp.where(fully_past, _dot_nt_3pass(kmean_ref[g], q), NEG_INF)
        sel = jnp.zeros((nbp, blk), jnp.bool_)
        for _ in range(MOBA_TOPK):
            m = jnp.max(gate, axis=0, keepdims=True)
            first = jnp.min(jnp.where(gate == m, bid_f, float(nbp)), axis=0, keepdims=True)
            pick = bid_f == first
            sel = sel | pick
            gate = jnp.where(pick, NEG_INF, gate)

        s_hi, s_mid, s_lo = (p.astype(F32) for p in _split3(slope))
        slope_rows = jnp.where(part % 3 == 0, s_hi, jnp.where(part % 3 == 1, s_mid, s_lo))
        slope_rows = jnp.where(part < 6, slope_rows, 0.0)
        aug_ref[g] = jnp.zeros(aug_ref.shape[1:], F32)
        aug_ref[g, 0:nbp, :] = jnp.where(sel & fully_past, 0.0, MASKED)
        aug_ref[g, nbp:nbp + SUBLANES, :] = slope_rows
        q_ext = aug_ref[g].T
        qs = (q * hd ** -0.5).astype(BF16)
        q_aug = jnp.concatenate([qs, q_ext.astype(BF16)], axis=1)
        q_own = jnp.concatenate([qs, jnp.where(lane < nbp, 0.0, q_ext).astype(BF16)], axis=1)
        s_own = jnp.where(col <= row, _dot_nt(q_own, kaug_ref[g, own, :]), MASKED)
        return q_aug, s_own, vaug_ref[g, own, :]

    prepared = [prepare(g) for g in range(group)]

    def finish(n_past):
        scores = [[_dot(q_aug, kaug_t_ref[g, :, n * blk:(n + 1) * blk]) for n in range(n_past)]
                  for g, (q_aug, _, _) in enumerate(prepared)]
        for g, (_, s_own, v_own) in enumerate(prepared):
            m_run = s_own
            for s_n in scores[g]:
                m_run = jnp.maximum(m_run, s_n)
            m = jnp.max(m_run, axis=-1, keepdims=True)
            acc = _dot(jnp.exp(s_own - m).astype(BF16), v_own)
            for n, s_n in enumerate(scores[g]):
                acc = acc + _dot(jnp.exp(s_n - m).astype(BF16), vaug_ref[g, n * blk:(n + 1) * blk, :])
            o_ref[0, :, g * hd:(g + 1) * hd] = (acc[:, 0:hd] / acc[:, hd:hd + 1]).astype(o_ref.dtype)

    @pl.when(qi == 0)
    def _():
        finish(0)

    lo = 0
    while lo < nb - 1:
        hi = min(lo + MOBA_VARIANT_BLOCKS, nb - 1)

        @pl.when((qi > lo) & (qi <= hi))
        def _(hi=hi):
            finish(hi)

        lo = hi


def _moba_prompt(q, k, v):
    b, t, width = q.shape
    blk, hd, group = MOBA_BLOCK, LANES, MOBA_HEAD_GROUP
    heads = width // hd
    assert t % blk == 0 and width % hd == 0 and heads % group == 0
    nb = t // blk
    nbp = -(-nb // SUBLANES) * SUBLANES
    assert nb >= MOBA_TOPK and nbp + SUBLANES <= LANES
    return pl.pallas_call(
        functools.partial(_moba_prompt_kernel, nb=nb, nbp=nbp, hd=hd, heads=heads),
        grid=(b, heads // group, nb),
        in_specs=[pl.BlockSpec((1, blk, group * hd), lambda i, h, j: (i, j, h)),
                  pl.BlockSpec((1, t, group * hd), lambda i, h, j: (i, 0, h)),
                  pl.BlockSpec((1, t, group * hd), lambda i, h, j: (i, 0, h))],
        out_specs=pl.BlockSpec((1, blk, group * hd), lambda i, h, j: (i, j, h)),
        out_shape=jax.ShapeDtypeStruct((b, t, width), BF16),
        scratch_shapes=[pltpu.VMEM((group, nbp, hd), F32),
                        pltpu.VMEM((group, t, 2 * hd), BF16),
                        pltpu.VMEM((group, 2 * hd, t), BF16),
                        pltpu.VMEM((group, t, 2 * hd), BF16),
                        pltpu.VMEM((group, LANES, blk), F32)],
        compiler_params=_cparams("parallel", "parallel", "arbitrary"),
        name="moba_prompt",
    )(q, k, v)


PAGES_PER_STEP = 16


def _block_sum_kernel(pt_ref, *refs, page, blk):
    del pt_ref
    k_refs, o_ref = refs[:-1], refs[-1]
    per_blk = blk // page
    for n in range(len(k_refs) // per_blk):
        acc = jnp.sum(k_refs[n * per_blk][...], axis=0)
        for r in range(1, per_blk):
            acc = acc + jnp.sum(k_refs[n * per_blk + r][...], axis=0)
        o_ref[0, n] = acc


def _paged_block_sums(cache, layer, page_table):
    _, _, page, heads, hd = cache.shape
    b, n_pages = page_table.shape
    pps = PAGES_PER_STEP
    assert MOBA_BLOCK % page == 0 and (pps * page) % MOBA_BLOCK == 0 and n_pages % pps == 0
    blocks_per_step = pps * page // MOBA_BLOCK

    def page_spec(r):
        return pl.BlockSpec((None, None, page, heads, hd),
                            lambda i, s, pt: (layer, pt[i, s * pps + r], 0, 0, 0))

    return pl.pallas_call(
        functools.partial(_block_sum_kernel, page=page, blk=MOBA_BLOCK),
        grid_spec=pltpu.PrefetchScalarGridSpec(
            num_scalar_prefetch=1,
            grid=(b, n_pages // pps),
            in_specs=[page_spec(r) for r in range(pps)],
            out_specs=pl.BlockSpec((1, blocks_per_step, heads, hd), lambda i, s, pt: (i, s, 0, 0)),
        ),
        out_shape=jax.ShapeDtypeStruct((b, n_pages * page // MOBA_BLOCK, heads, hd), F32),
        compiler_params=_cparams("parallel", "arbitrary"),
        name="paged_block_sums",
    )(page_table, *([cache] * pps))


def _decode_select_kernel(q_ref, ks_ref, o_ref, *, nb):
    gate = jnp.sum(ks_ref[0] * q_ref[0][None], axis=-1, keepdims=True) / MOBA_BLOCK
    idx = lax.broadcasted_iota(jnp.int32, gate.shape, 0)
    for r in range(MOBA_TOPK):
        m = jnp.max(gate, axis=0, keepdims=True)
        first = jnp.min(jnp.where(gate == m, idx, nb), axis=0, keepdims=True)
        o_ref[0, r] = jnp.broadcast_to(first[0], o_ref.shape[2:])
        gate = jnp.where(idx == first, NEG_INF, gate)


def _decode_select(q, ksum):
    b, nb, heads, hd = ksum.shape
    assert nb >= MOBA_TOPK
    out = pl.pallas_call(
        functools.partial(_decode_select_kernel, nb=nb),
        grid=(b,),
        in_specs=[pl.BlockSpec((1, heads, hd), lambda i: (i, 0, 0)),
                  pl.BlockSpec((1, nb, heads, hd), lambda i: (i, 0, 0, 0))],
        out_specs=pl.BlockSpec((1, MOBA_TOPK, heads, LANES), lambda i: (i, 0, 0, 0)),
        out_shape=jax.ShapeDtypeStruct((b, MOBA_TOPK, heads, LANES), jnp.int32),
        compiler_params=_cparams("parallel"),
        name="decode_select",
    )(q, ksum)
    return out[..., 0]


def _decode_attn_kernel(pt_ref, sel_ref, q_ref, kn_ref, vn_ref, *refs,
                        page, heads, past_len, per_blk):
    del pt_ref
    n_parts = MOBA_TOPK * per_blk
    k_refs, v_refs, o_ref = refs[:n_parts], refs[n_parts:2 * n_parts], refs[2 * n_parts]
    i, h = pl.program_id(0), pl.program_id(1)
    hd = q_ref.shape[2]
    q = q_ref[0, pl.ds(h, 1), :]
    scale = hd ** -0.5
    slope = jnp.exp2(-jnp.full((1, 1), h + 1, jnp.int32).astype(F32) * (8.0 / heads))

    m = jnp.sum(q * kn_ref[0, pl.ds(h, 1), :], axis=-1, keepdims=True) * scale
    l = jnp.ones_like(m)
    acc = vn_ref[0, pl.ds(h, 1), :]
    r = lax.broadcasted_iota(jnp.int32, (page, 1), 0)
    for j in range(MOBA_TOPK):
        n = sel_ref[(i * MOBA_TOPK + j) * heads + h]
        for part in range(per_blk):
            k_h = k_refs[j * per_blk + part][pl.ds(h, page, stride=heads), :]
            v_h = v_refs[j * per_blk + part][pl.ds(h, page, stride=heads), :]
            kpos = n * MOBA_BLOCK + part * page + r
            s = jnp.sum(k_h * q, axis=-1, keepdims=True) * scale - slope * (past_len - kpos).astype(F32)
            m_new = jnp.maximum(m, jnp.max(s, axis=0, keepdims=True))
            alpha = jnp.exp(m - m_new)
            p = jnp.exp(s - m_new)
            l = alpha * l + jnp.sum(p, axis=0, keepdims=True)
            acc = alpha * acc + jnp.sum(p * v_h, axis=0, keepdims=True)
            m = m_new
    o_ref[0, pl.ds(h, 1), :] = acc / l


def _moba_decode(q, k_new, v_new, cache_k, cache_v, layer, page_table, sel):
    b, heads, hd = q.shape
    n_layers, n_pool, page = cache_k.shape[:3]
    per_blk = MOBA_BLOCK // page
    past_len = page_table.shape[1] * page
    assert past_len % MOBA_BLOCK == 0
    rows = page * heads
    cache_k = cache_k.reshape(n_layers, n_pool, rows, hd)
    cache_v = cache_v.reshape(n_layers, n_pool, rows, hd)

    def page_spec(j, part):
        def index(i, h, pt, sl):
            n = sl[(i * MOBA_TOPK + j) * heads + h]
            return (layer, pt[i, n * per_blk + part], 0, 0)
        return pl.BlockSpec((None, None, rows, hd), index)

    tok_spec = pl.BlockSpec((1, heads, hd), lambda i, h, pt, sl: (i, 0, 0))
    page_specs = [page_spec(j, r) for j in range(MOBA_TOPK) for r in range(per_blk)]
    return pl.pallas_call(
        functools.partial(_decode_attn_kernel, page=page, heads=heads, past_len=past_len, per_blk=per_blk),
        grid_spec=pltpu.PrefetchScalarGridSpec(
            num_scalar_prefetch=2,
            grid=(b, heads),
            in_specs=[tok_spec, tok_spec, tok_spec] + page_specs * 2,
            out_specs=tok_spec,
        ),
        out_shape=jax.ShapeDtypeStruct((b, heads, hd), F32),
        compiler_params=_cparams("parallel", "arbitrary"),
        name="moba_decode",
    )(page_table, sel.reshape(-1), q, k_new, v_new,
      *([cache_k] * len(page_specs)), *([cache_v] * len(page_specs)))


def _gla_gate_kernel(x_ref, g_ref, wa_ref, w2_ref, b_ref, o_ref):
    hn = _rmsnorm(x_ref[...], g_ref[...])
    a = _wdot(hn, wa_ref[...], w_rows=True)
    z = _wdot(a, w2_ref[...]) + b_ref[...]
    o_ref[...] = jax.nn.log_sigmoid(z) / GLA_GATE_TAU


def _gla_gate(x, g, wa, w2, bias, layer, wlayer, *, tm=512):
    m, d = x.shape
    rank, n = w2.shape[1:]
    tm = _tile(m, tm)
    return pl.pallas_call(
        _gla_gate_kernel,
        grid=(m // tm,),
        in_specs=[pl.BlockSpec((tm, d), lambda i: (i, 0)),
                  _layer_spec((1, d), layer, lambda i: (0, 0)),
                  _layer_spec((rank, d), wlayer, lambda i: (0, 0)),
                  _layer_spec((rank, n), wlayer, lambda i: (0, 0)),
                  _layer_spec((1, n), wlayer, lambda i: (0, 0))],
        out_specs=pl.BlockSpec((tm, n), lambda i: (i, 0)),
        out_shape=jax.ShapeDtypeStruct((m, n), F32),
        compiler_params=_cparams("parallel"),
        name="gla_gate",
    )(x, _gain(g), wa, w2, _gain(bias))


def _gla_prompt_kernel(z_ref, lg_ref, gg_ref, o_ref, s_ref, *, heads, dk, dv):
    c = GLA_CHUNK

    @pl.when(pl.program_id(0) == 0)
    def _():
        s_ref[...] = jnp.zeros_like(s_ref)

    row = lax.broadcasted_iota(jnp.int32, (c, c), 0)
    col = lax.broadcasted_iota(jnp.int32, (c, c), 1)
    causal = col <= row
    tri = jnp.where(causal, 1.0, 0.0).astype(BF16)
    k0, v0, r0 = heads * dk, 2 * heads * dk, 2 * heads * dk + heads * dv
    streams = [(bi, h) for bi in range(z_ref.shape[0]) for h in range(heads)]

    cums = []
    for bi, h in streams:
        lg_hi, lg_mid, lg_lo = _split3(lg_ref[bi, :, h * dk:(h + 1) * dk])
        cums.append(_dot(tri, lg_hi) + (_dot(tri, lg_mid) + _dot(tri, lg_lo)))
    decs = []
    for (bi, h), b in zip(streams, cums):
        q = z_ref[bi, :, h * dk:(h + 1) * dk] * dk ** -0.5
        k = z_ref[bi, :, k0 + h * dk:k0 + (h + 1) * dk]
        b_last = b[c - 1:c, :]
        decs.append(((q * jnp.exp(b)).astype(BF16), (k * jnp.exp(-b)).astype(BF16),
                     (k * jnp.exp(b_last - b)).astype(BF16), b_last))
    attn = [jnp.where(causal, _dot_nt(q_dec, k_dec), 0.0).astype(BF16) for q_dec, k_dec, _, _ in decs]
    outs = []
    for (bi, h), a, (q_dec, _, k_rem, b_last) in zip(streams, attn, decs):
        v = z_ref[bi, :, v0 + h * dv:v0 + (h + 1) * dv].astype(BF16)
        state = s_ref[bi, h]
        outs.append(_dot(a, v) + _dot(q_dec, state.astype(BF16)))
        decay = jnp.exp(jnp.broadcast_to(b_last, (LANES, dk))).T[:, 0:1]
        s_ref[bi, h] = decay * state + _dot_tn(k_rem, v)
    for (bi, h), o in zip(streams, outs):
        r = z_ref[bi, :, r0 + h * dv:r0 + (h + 1) * dv]
        o_ref[bi, :, h * dv:(h + 1) * dv] = (
            _rmsnorm(o, gg_ref[...]) * (r * jax.nn.sigmoid(r))).astype(o_ref.dtype)


def _gla_prompt(z, lg, g_gla, layer, *, heads, dk, dv):
    b, t, zw = z.shape
    c = GLA_CHUNK
    assert t % c == 0 and zw == 2 * heads * (dk + dv)
    return pl.pallas_call(
        functools.partial(_gla_prompt_kernel, heads=heads, dk=dk, dv=dv),
        grid=(t // c,),
        in_specs=[pl.BlockSpec((b, c, zw), lambda j: (0, j, 0)),
                  pl.BlockSpec((b, c, heads * dk), lambda j: (0, j, 0)),
                  _layer_spec((1, dv), layer, lambda j: (0, 0))],
        out_specs=[pl.BlockSpec((b, c, heads * dv), lambda j: (0, j, 0)),
                   pl.BlockSpec((b, heads, dk, dv), lambda j: (0, 0, 0, 0))],
        out_shape=[jax.ShapeDtypeStruct((b, t, heads * dv), BF16),
                   jax.ShapeDtypeStruct((b, heads, dk, dv), F32)],
        compiler_params=_cparams("arbitrary"),
        name="gla_prompt",
    )(z, lg, _gain(g_gla))


def _gla_decode_kernel(q_ref, k_ref, lg_ref, v_ref, r_ref, gg_ref, s0_ref, o_ref, s_ref, *, dk):
    q = q_ref[0, 0] * dk ** -0.5
    k = k_ref[0, 0]
    b = lg_ref[0, 0]
    v = v_ref[0, 0]
    state = s0_ref[0, 0]
    q_dec = q * jnp.exp(b)
    a = jnp.sum(q_dec * (k * jnp.exp(-b)), axis=0, keepdims=True)
    o = a * v + jnp.sum(q_dec * state, axis=0, keepdims=True)
    s_ref[0, 0] = jnp.exp(b) * state + (k * jnp.exp(b - b)) * v
    r = r_ref[0, 0]
    o_ref[0, 0] = _rmsnorm(o, gg_ref[...]) * (r * jax.nn.sigmoid(r))


def _gla_decode(q, k, lg, v, r, g_gla, s0, layer):
    _, b, heads, dk, dv = s0.shape
    col = pl.BlockSpec((1, 1, dk, 1), lambda i, h: (i, h, 0, 0))
    rowspec = pl.BlockSpec((1, 1, 1, dv), lambda i, h: (i, h, 0, 0))
    st = pl.BlockSpec((1, 1, dk, dv), lambda i, h: (i, h, 0, 0))
    return pl.pallas_call(
        functools.partial(_gla_decode_kernel, dk=dk),
        grid=(b, heads),
        in_specs=[col, col, col, rowspec, rowspec,
                  _layer_spec((1, dv), layer, lambda i, h: (0, 0)),
                  _layer_spec((1, 1, dk, dv), layer, lambda i, h: (i, h, 0, 0))],
        out_specs=[rowspec, st],
        out_shape=[jax.ShapeDtypeStruct((b, heads, 1, dv), F32),
                   jax.ShapeDtypeStruct((b, heads, dk, dv), s0.dtype)],
        compiler_params=_cparams("parallel", "parallel"),
        name="gla_decode",
    )(q, k, lg, v, r, _gain(g_gla), s0)


def _trunk(x, p, start, pool_state, gla_state, cache_k, cache_v, page_table, wts):
    b, t, d = x.shape
    m = b * t
    depth = wts["g_mix"].shape[0]
    pool_w = wts["pool_scale"].shape[1]
    pool_buf = max(POOL_WINDOWS) - 1
    heads_m, hd = cache_k.shape[3], cache_k.shape[4]
    moba_w = heads_m * hd
    _, _, heads_g, dk, dv = gla_state.shape
    dk_tot, dv_tot = heads_g * dk, heads_g * dv
    decode = page_table is not None
    assert pool_w == moba_w
    p = p.reshape(depth, m, -1)

    h = x.reshape(m, d)
    pools, ks, vs, glas, ffn_bf16 = [], [], [], [], []
    for i in range(depth):
        j = i // 2
        if i % 2 == 0:
            u, q, k, v = (a.reshape(b, t, -1) for a in _norm_matmul(
                h, wts["g_mix"], wts["w_in_even"], i, j, pool_w + 3 * moba_w, n_out=4, tm=512, tn=1024))
            if decode:
                buf = pool_state[j]
                pools.append(jnp.concatenate([buf, u], axis=1)[:, -pool_buf:])
            else:
                buf = jnp.zeros((b, pool_buf, pool_w), x.dtype)
                pools.append(u[:, -pool_buf:])
            y_a = _pool_mixer(u, buf, start, wts["w_pool"], wts["pool_scale"], j)
            if decode:
                q3, k3, v3 = (a.reshape(b, heads_m, hd) for a in (q, k, v))
                ksum = _paged_block_sums(cache_k, j, page_table)
                sel = _decode_select(q3, ksum)
                y_b = _moba_decode(q3, k3, v3, cache_k, cache_v, j, page_table, sel)
            else:
                y_b = _moba_prompt(q, k, v)
            h = _proj_res([y_a.reshape(m, pool_w), y_b.reshape(m, moba_w)], wts["w_out_even"], j, h)
            ks.append(k.reshape(b, t, heads_m, hd))
            vs.append(v.reshape(b, t, heads_m, hd))
        else:
            n_main = 2 * dk_tot + 2 * dv_tot
            z = _norm_matmul(h, wts["g_mix"], wts["w_in_odd"], i, j, n_main, w_rows=True, tn=1024)
            lg = _gla_gate(h, wts["g_mix"], wts["w_gate_odd"], wts["w_gk2"], wts["b_gk"], i, j)
            if decode:
                def cols(a):
                    return a.reshape(b, heads_g, dk, 1)
                def rows(a):
                    return a.reshape(b, heads_g, 1, dv)
                og, s_new = _gla_decode(
                    cols(z[:, :dk_tot]), cols(z[:, dk_tot:2 * dk_tot]), cols(lg),
                    rows(z[:, 2 * dk_tot:2 * dk_tot + dv_tot]), rows(z[:, 2 * dk_tot + dv_tot:]),
                    wts["g_gla"], gla_state, j)
            else:
                og, s_new = _gla_prompt(z.reshape(b, t, n_main), lg.reshape(b, t, dk_tot), wts["g_gla"], j,
                                        heads=heads_g, dk=dk, dv=dv)
            h = _proj_res([og.reshape(m, dv_tot)], wts["w_out_odd"], j, h)
            glas.append(s_new.astype(gla_state.dtype))
        if decode:
            h, w16 = _ffn(h, wts["g_ffn"], (wts["w_ffn_gate"], wts["w_ffn_up"], wts["w_ffn_down"]), i,
                          emit_bf16=True)
            ffn_bf16.append(w16)
        else:
            h = _ffn(h, wts["g_ffn"], wts["ffn_bf16"][i], i)
        h = _ple(h, wts["g_ple"], wts["w_ple_gate"], p, wts["w_ple_proj"], i)
    y = _final_norm(h, wts["g_final"]).reshape(b, t, d)
    return (y, jnp.stack(pools), jnp.stack(ks), jnp.stack(vs), jnp.stack(glas)), ffn_bf16


_MATMUL_WEIGHTS = ("w_in_even", "w_pool", "w_out_even", "w_in_odd", "w_gk2", "w_out_odd",
                   "w_ple_gate", "w_ple_proj")


def kernel(x_prompt, x_sample, state_pool, cache_k, cache_v, state_gla, page_table, p_prompt, p_sample,
           g_mix, g_ffn, g_ple, g_final, w_in_even, w_pool, pool_scale, w_out_even,
           w_in_odd, w_gk2, b_gk, g_gla, w_out_odd, w_ffn_gate, w_ffn_up, w_ffn_down,
           w_ple_gate, w_ple_proj):
    wts = dict(g_mix=g_mix, g_ffn=g_ffn, g_ple=g_ple, g_final=g_final, w_in_even=w_in_even,
               w_pool=w_pool, pool_scale=pool_scale, w_out_even=w_out_even, w_in_odd=w_in_odd,
               w_gk2=w_gk2, b_gk=b_gk, g_gla=g_gla, w_out_odd=w_out_odd, w_ffn_gate=w_ffn_gate,
               w_ffn_up=w_ffn_up, w_ffn_down=w_ffn_down, w_ple_gate=w_ple_gate, w_ple_proj=w_ple_proj)
    n_main = 2 * state_gla.shape[2] * (state_gla.shape[3] + state_gla.shape[4])
    wts["w_in_odd"] = jnp.swapaxes(w_in_odd, 1, 2)
    wts["w_gate_odd"] = wts["w_in_odd"][:, n_main:, :]
    past_len = page_table.shape[1] * cache_k.shape[2]
    (y_s, pool_s, k_s, v_s, gla_s), ffn_bf16 = _trunk(x_sample, p_sample, past_len, state_pool, state_gla,
                                                      cache_k, cache_v, page_table, wts)
    wts_bf16 = dict(wts, ffn_bf16=ffn_bf16)
    for name in _MATMUL_WEIGHTS + ("w_gate_odd",):
        wts_bf16[name] = wts[name].astype(BF16)
    gla0 = jax.ShapeDtypeStruct((state_gla.shape[0], x_prompt.shape[0]) + state_gla.shape[2:], state_gla.dtype)
    (y_p, pool_p, k_p, v_p, gla_p), _ = _trunk(x_prompt, p_prompt, 0, None, gla0, cache_k, cache_v, None, wts_bf16)
    return (y_p, y_s, pool_p, pool_s, k_p, k_s, v_p, v_s, gla_p, gla_s)
```

```python
import functools

import jax
import jax.numpy as jnp
from jax import lax
from jax.experimental import pallas as pl
from jax.experimental.pallas import tpu as pltpu

F32 = jnp.float32
BF16 = jnp.bfloat16

RMS_EPS = 1e-6
POOL_WINDOWS = (2, 4, 8, 16)
POOL_HALO = 16
MOBA_BLOCK = 256
MOBA_TOPK = 3
MOBA_VARIANT_BLOCKS = 4
MOBA_HEAD_GROUP = 2
GLA_GATE_TAU = 16.0
GLA_CHUNK = 64
LANES = 128
SUBLANES = 8
VMEM_LIMIT = 52 * 1024 * 1024
NEG_INF = float("-inf")
MASKED = -1e30


def _cparams(*sem):
    return pltpu.CompilerParams(dimension_semantics=sem, vmem_limit_bytes=VMEM_LIMIT)


def _rmsnorm(x, g):
    ms = jnp.mean(x * x, axis=-1, keepdims=True)
    return x * lax.rsqrt(ms + RMS_EPS) * g


def _dot(a, b):
    return jnp.dot(a, b, preferred_element_type=F32)


def _wdot(a, w, w_rows=False):
    dims = (((1,), (1 if w_rows else 0,)), ((), ()))
    if w.dtype == F32:
        return lax.dot_general(a.astype(F32), w, dims, precision=lax.Precision.HIGHEST,
                               preferred_element_type=F32)
    return lax.dot_general(a.astype(BF16), w, dims, preferred_element_type=F32)


def _dot_nt(a, b):
    return lax.dot_general(a, b, (((1,), (1,)), ((), ())), preferred_element_type=F32)


def _dot_tn(a, b):
    return lax.dot_general(a, b, (((0,), (0,)), ((), ())), preferred_element_type=F32)


def _split2(a):
    hi = a.astype(BF16)
    lo = (a - hi.astype(F32)).astype(BF16)
    return hi, lo


def _split3(a):
    hi = a.astype(BF16)
    r = a - hi.astype(F32)
    mid = r.astype(BF16)
    lo = (r - mid.astype(F32)).astype(BF16)
    return hi, mid, lo


def _dot_nt_3pass(a, b):
    ah, al = _split2(a)
    bh, bl = _split2(b)
    return _dot_nt(ah, bh) + (_dot_nt(ah, bl) + _dot_nt(al, bh))


def _tile(m, t):
    t = min(m, t)
    assert m % t == 0, (m, t)
    return t


def _act_dtype(rows):
    return BF16 if rows % 16 == 0 else F32


def _layer_spec(tail, layer, index):
    return pl.BlockSpec((None,) + tuple(tail), lambda *a: (layer,) + tuple(index(*a)))


def _gain(g):
    return g.reshape(g.shape[0], 1, g.shape[1])


def _norm_matmul_kernel(x_ref, g_ref, w_ref, *refs, steps_per_out, w_rows):
    o_refs, hn_ref = refs[:-1], refs[-1]
    j = pl.program_id(1)

    @pl.when(j == 0)
    def _():
        hn_ref[...] = _rmsnorm(x_ref[...], g_ref[...]).astype(hn_ref.dtype)

    res = _wdot(hn_ref[...], w_ref[...], w_rows)
    if len(o_refs) == 1:
        o_refs[0][...] = res
    else:
        for idx, o_ref in enumerate(o_refs):
            @pl.when(j // steps_per_out == idx)
            def _(o_ref=o_ref):
                o_ref[...] = res


def _norm_matmul(x, g, w, layer, wlayer, n, *, n_out=1, w_rows=False, tm=1024, tn=512):
    m, d = x.shape
    group = n // n_out
    tm, tn = _tile(m, tm), _tile(group, tn)
    spo = group // tn

    def out_index(idx):
        return lambda i, j: (i, jnp.clip(j - idx * spo, 0, spo - 1))

    outs = pl.pallas_call(
        functools.partial(_norm_matmul_kernel, steps_per_out=spo, w_rows=w_rows),
        grid=(m // tm, n // tn),
        in_specs=[pl.BlockSpec((tm, d), lambda i, j: (i, 0)),
                  _layer_spec((1, d), layer, lambda i, j: (0, 0)),
                  (_layer_spec((tn, d), wlayer, lambda i, j: (j, 0)) if w_rows else
                   _layer_spec((d, tn), wlayer, lambda i, j: (0, j)))],
        out_specs=[pl.BlockSpec((tm, tn), out_index(idx)) for idx in range(n_out)],
        out_shape=[jax.ShapeDtypeStruct((m, group), F32)] * n_out,
        scratch_shapes=[pltpu.VMEM((tm, d), _act_dtype(tm))],
        compiler_params=_cparams("parallel", "arbitrary"),
        name="norm_matmul",
    )(x, _gain(g), w)
    return outs[0] if n_out == 1 else outs


def _proj_res_kernel(*refs, n_in):
    a_refs, w_refs = refs[:n_in], refs[n_in:2 * n_in]
    res_ref, o_ref = refs[2 * n_in], refs[2 * n_in + 1]
    acc = res_ref[...]
    for a_ref, w_ref in zip(a_refs, w_refs):
        acc = acc + _wdot(a_ref[...], w_ref[...])
    o_ref[...] = acc


def _proj_res(a_list, w, layer, res, *, tm=1024, tn=512):
    m, n = res.shape
    tm, tn = _tile(m, tm), _tile(n, tn)
    n_in = len(a_list)
    kw = a_list[0].shape[1]
    assert all(a.shape[1] == kw for a in a_list) and kw * n_in == w.shape[1]
    in_specs = ([pl.BlockSpec((tm, kw), lambda i, j: (i, 0)) for _ in a_list]
                + [_layer_spec((kw, tn), layer, lambda i, j, r=r: (r, j)) for r in range(n_in)]
                + [pl.BlockSpec((tm, tn), lambda i, j: (i, j))])
    return pl.pallas_call(
        functools.partial(_proj_res_kernel, n_in=n_in),
        grid=(m // tm, n // tn),
        in_specs=in_specs,
        out_specs=pl.BlockSpec((tm, tn), lambda i, j: (i, j)),
        out_shape=jax.ShapeDtypeStruct((m, n), F32),
        compiler_params=_cparams("parallel", "arbitrary"),
        name="proj_res",
    )(*a_list, *([w] * n_in), res)


def _ffn_kernel(x_ref, g_ref, wg_ref, wu_ref, wd_ref, o_ref, *rest):
    cast_refs, hn_ref = rest[:-1], rest[-1]

    @pl.when(pl.program_id(1) == 0)
    def _():
        x = x_ref[...]
        hn_ref[...] = _rmsnorm(x, g_ref[...]).astype(hn_ref.dtype)
        o_ref[...] = x

    hn = hn_ref[...]
    a = _wdot(hn, wg_ref[...])
    b = _wdot(hn, wu_ref[...])
    h = (a * jax.nn.sigmoid(a)) * b
    o_ref[...] += _wdot(h, wd_ref[...])
    for w_ref, c_ref in zip((wg_ref, wu_ref, wd_ref), cast_refs):
        c_ref[...] = w_ref[...].astype(c_ref.dtype)


def _ffn(x, g, w3, layer, *, emit_bf16=False, tm=512, tf=512):
    m, d = x.shape
    wg, wu, wd = w3
    stacked = wg.ndim == 3
    f = wg.shape[-1]
    tm, tf = _tile(m, tm), _tile(f, tf)
    assert not emit_bf16 or m == tm

    def wspec(tail, index):
        return _layer_spec(tail, layer, index) if stacked else pl.BlockSpec(tail, index)

    up_spec = ((d, tf), lambda i, j: (0, j))
    down_spec = ((tf, d), lambda i, j: (j, 0))
    out_specs = [pl.BlockSpec((tm, d), lambda i, j: (i, 0))]
    out_shape = [jax.ShapeDtypeStruct((m, d), F32)]
    if emit_bf16:
        out_specs += [pl.BlockSpec(*up_spec), pl.BlockSpec(*up_spec), pl.BlockSpec(*down_spec)]
        out_shape += [jax.ShapeDtypeStruct((d, f), BF16), jax.ShapeDtypeStruct((d, f), BF16),
                      jax.ShapeDtypeStruct((f, d), BF16)]
    outs = pl.pallas_call(
        _ffn_kernel,
        grid=(m // tm, f // tf),
        in_specs=[pl.BlockSpec((tm, d), lambda i, j: (i, 0)),
                  _layer_spec((1, d), layer, lambda i, j: (0, 0)),
                  wspec(*up_spec), wspec(*up_spec), wspec(*down_spec)],
        out_specs=out_specs,
        out_shape=out_shape,
        scratch_shapes=[pltpu.VMEM((tm, d), _act_dtype(tm))],
        compiler_params=_cparams("parallel", "arbitrary"),
        name="ffn",
    )(x, _gain(g), wg, wu, wd)
    return (outs[0], tuple(outs[1:])) if emit_bf16 else outs[0]


def _ple_kernel(x_ref, g_ref, wg_ref, p_ref, wp_ref, o_ref, hn_ref):
    j = pl.program_id(1)
    tn = o_ref.shape[1]

    @pl.when(j == 0)
    def _():
        hn_ref[...] = _rmsnorm(x_ref[...], g_ref[...]).astype(hn_ref.dtype)

    gate = _wdot(hn_ref[...], wg_ref[...])
    proj = _wdot(p_ref[...], wp_ref[...])
    xj = x_ref[:, pl.ds(pl.multiple_of(j * tn, tn), tn)]
    o_ref[...] = xj + jax.nn.sigmoid(gate) * proj


def _ple(x, g, wg, p, wp, layer, *, tm=1024, tn=512):
    m, d = x.shape
    tm, tn = _tile(m, tm), _tile(d, tn)
    pd = p.shape[2]
    return pl.pallas_call(
        _ple_kernel,
        grid=(m // tm, d // tn),
        in_specs=[pl.BlockSpec((tm, d), lambda i, j: (i, 0)),
                  _layer_spec((1, d), layer, lambda i, j: (0, 0)),
                  _layer_spec((d, tn), layer, lambda i, j: (0, j)),
                  _layer_spec((tm, pd), layer, lambda i, j: (i, 0)),
                  _layer_spec((pd, tn), layer, lambda i, j: (0, j))],
        out_specs=pl.BlockSpec((tm, tn), lambda i, j: (i, j)),
        out_shape=jax.ShapeDtypeStruct((m, d), F32),
        scratch_shapes=[pltpu.VMEM((tm, d), _act_dtype(tm))],
        compiler_params=_cparams("parallel", "arbitrary"),
        name="ple",
    )(x, _gain(g), wg, p, wp)


def _final_norm_kernel(x_ref, g_ref, o_ref):
    o_ref[...] = _rmsnorm(x_ref[...], g_ref[...])


def _final_norm(x, g, *, tm=512):
    m, d = x.shape
    tm = _tile(m, tm)
    return pl.pallas_call(
        _final_norm_kernel,
        grid=(m // tm,),
        in_specs=[pl.BlockSpec((tm, d), lambda i: (i, 0)),
                  pl.BlockSpec((1, d), lambda i: (0, 0))],
        out_specs=pl.BlockSpec((tm, d), lambda i: (i, 0)),
        out_shape=jax.ShapeDtypeStruct((m, d), F32),
        compiler_params=_cparams("parallel"),
        name="final_norm",
    )(x, g.reshape(1, d))


def _pool_kernel(u_ref, halo_ref, w_ref, s_ref, o_ref, ext_ref, *, tt, start, gw):
    t = pl.program_id(1)

    @pl.when(t == 0)
    def _():
        ext_ref[0:POOL_HALO, :] = halo_ref[0]

    @pl.when(t > 0)
    def _():
        ext_ref[0:POOL_HALO, :] = ext_ref[tt:tt + POOL_HALO, :]

    ext_ref[POOL_HALO:POOL_HALO + tt, :] = u_ref[0]
    pos = start + t * tt + lax.broadcasted_iota(jnp.int32, (tt, 1), 0)
    for g, w in enumerate(POOL_WINDOWS):
        c0, c1 = g * gw, (g + 1) * gw
        cur = ext_ref[POOL_HALO:POOL_HALO + tt, c0:c1]
        acc = cur
        for k in range(1, w):
            acc = acc + ext_ref[POOL_HALO - k:POOL_HALO - k + tt, c0:c1]
        cnt = jnp.minimum(pos + 1, w).astype(F32)
        d = acc / cnt - cur
        y = _wdot(d, w_ref[g])
        o_ref[0, :, c0:c1] = (y * s_ref[:, c0:c1]).astype(o_ref.dtype)


def _pool_mixer(u, buf, start, w_pool, scale, layer, *, tt=512):
    b, t, width = u.shape
    tt = _tile(t, tt)
    gw = width // len(POOL_WINDOWS)
    halo = jnp.pad(buf, ((0, 0), (POOL_HALO - buf.shape[1], 0), (0, 0)))
    return pl.pallas_call(
        functools.partial(_pool_kernel, tt=tt, start=start, gw=gw),
        grid=(b, t // tt),
        in_specs=[pl.BlockSpec((1, tt, width), lambda i, j: (i, j, 0)),
                  pl.BlockSpec((1, POOL_HALO, width), lambda i, j: (i, 0, 0)),
                  _layer_spec(w_pool.shape[1:], layer, lambda i, j: (0, 0, 0)),
                  _layer_spec((1, width), layer, lambda i, j: (0, 0))],
        out_specs=pl.BlockSpec((1, tt, width), lambda i, j: (i, j, 0)),
        out_shape=jax.ShapeDtypeStruct((b, t, width), _act_dtype(tt)),
        scratch_shapes=[pltpu.VMEM((POOL_HALO + max(tt, POOL_HALO), width), F32)],
        compiler_params=_cparams("parallel", "arbitrary"),
        name="pool_mixer",
    )(u, halo, w_pool, _gain(scale))


def _moba_prompt_kernel(q_ref, k_ref, v_ref, o_ref, kmean_ref, kaug_ref, kaug_t_ref, vaug_ref, aug_ref,
                        *, nb, nbp, hd, heads):
    qi = pl.program_id(2)
    blk = MOBA_BLOCK
    group = kmean_ref.shape[0]
    lane = lax.broadcasted_iota(jnp.int32, (blk, LANES), 1)

    @pl.when(qi == 0)
    def _():
        kmean_ref[...] = jnp.zeros_like(kmean_ref)
        offs = lax.broadcasted_iota(jnp.int32, (blk, LANES), 0).astype(F32)
        ones_col = jnp.where(lane == 0, 1.0, 0.0).astype(BF16)
        for g in range(group):
            cols = slice(g * hd, (g + 1) * hd)
            for n in range(nb):
                rows = slice(n * blk, (n + 1) * blk)
                kb = k_ref[0, rows, cols]
                kmean_ref[g, n:n + 1, :] = jnp.sum(kb, axis=0, keepdims=True) / blk
                ext = jnp.where(lane == n, 1.0, 0.0)
                ext = jnp.where((lane >= nbp) & (lane < nbp + 3), float(n * blk), ext)
                ext = jnp.where((lane >= nbp + 3) & (lane < nbp + 6), offs, ext)
                kaug_ref[g, rows, 0:hd] = kb.astype(BF16)
                kaug_ref[g, rows, hd:] = ext.astype(BF16)
                kaug_t_ref[g, 0:hd, rows] = kb.T.astype(BF16)
                kaug_t_ref[g, hd:, rows] = ext.T.astype(BF16)
                vaug_ref[g, rows, 0:hd] = v_ref[0, rows, cols].astype(BF16)
                vaug_ref[g, rows, hd:] = ones_col

    bid = lax.broadcasted_iota(jnp.int32, (nbp, blk), 0)
    bid_f = bid.astype(F32)
    fully_past = bid < qi
    part = lax.broadcasted_iota(jnp.int32, (SUBLANES, blk), 0)
    own = pl.ds(pl.multiple_of(qi * blk, blk), blk)
    row = lax.broadcasted_iota(jnp.int32, (blk, blk), 0)
    col = lax.broadcasted_iota(jnp.int32, (blk, blk), 1)

    def prepare(g):
        h = pl.program_id(1) * group + g
        slope = jnp.exp2(-jnp.full((1, 1), h + 1, jnp.int32).astype(F32) * (8.0 / heads))
        q = q_ref[0, :, g * hd:(g + 1) * hd]

        gate = jnp.where(fully_past, _dot_nt_3pass(kmean_ref[g], q), NEG_INF)
        sel = jnp.zeros((nbp, blk), jnp.bool_)
        for _ in range(MOBA_TOPK):
            m = jnp.max(gate, axis=0, keepdims=True)
            first = jnp.min(jnp.where(gate == m, bid_f, float(nbp)), axis=0, keepdims=True)
            pick = bid_f == first
            sel = sel | pick
            gate = jnp.where(pick, NEG_INF, gate)

        s_hi, s_mid, s_lo = (p.astype(F32) for p in _split3(slope))
        slope_rows = jnp.where(part % 3 == 0, s_hi, jnp.where(part % 3 == 1, s_mid, s_lo))
        slope_rows = jnp.where(part < 6, slope_rows, 0.0)
        aug_ref[g] = jnp.zeros(aug_ref.shape[1:], F32)
        aug_ref[g, 0:nbp, :] = jnp.where(sel & fully_past, 0.0, MASKED)
        aug_ref[g, nbp:nbp + SUBLANES, :] = slope_rows
        q_ext = aug_ref[g].T
        qs = (q * hd ** -0.5).astype(BF16)
        q_aug = jnp.concatenate([qs, q_ext.astype(BF16)], axis=1)
        q_own = jnp.concatenate([qs, jnp.where(lane < nbp, 0.0, q_ext).astype(BF16)], axis=1)
        s_own = jnp.where(col <= row, _dot_nt(q_own, kaug_ref[g, own, :]), MASKED)
        return q_aug, s_own, vaug_ref[g, own, :]

    prepared = [prepare(g) for g in range(group)]

    def finish(n_past):
        scores = [[_dot(q_aug, kaug_t_ref[g, :, n * blk:(n + 1) * blk]) for n in range(n_past)]
                  for g, (q_aug, _, _) in enumerate(prepared)]
        for g, (_, s_own, v_own) in enumerate(prepared):
            m_run = s_own
            for s_n in scores[g]:
                m_run = jnp.maximum(m_run, s_n)
            m = jnp.max(m_run, axis=-1, keepdims=True)
            acc = _dot(jnp.exp(s_own - m).astype(BF16), v_own)
            for n, s_n in enumerate(scores[g]):
                acc = acc + _dot(jnp.exp(s_n - m).astype(BF16), vaug_ref[g, n * blk:(n + 1) * blk, :])
            o_ref[0, :, g * hd:(g + 1) * hd] = (acc[:, 0:hd] / acc[:, hd:hd + 1]).astype(o_ref.dtype)

    @pl.when(qi == 0)
    def _():
        finish(0)

    lo = 0
    while lo < nb - 1:
        hi = min(lo + MOBA_VARIANT_BLOCKS, nb - 1)

        @pl.when((qi > lo) & (qi <= hi))
        def _(hi=hi):
            finish(hi)

        lo = hi


def _moba_prompt(q, k, v):
    b, t, width = q.shape
    blk, hd, group = MOBA_BLOCK, LANES, MOBA_HEAD_GROUP
    heads = width // hd
    assert t % blk == 0 and width % hd == 0 and heads % group == 0
    nb = t // blk
    nbp = -(-nb // SUBLANES) * SUBLANES
    assert nb >= MOBA_TOPK and nbp + SUBLANES <= LANES
    return pl.pallas_call(
        functools.partial(_moba_prompt_kernel, nb=nb, nbp=nbp, hd=hd, heads=heads),
        grid=(b, heads // group, nb),
        in_specs=[pl.BlockSpec((1, blk, group * hd), lambda i, h, j: (i, j, h)),
                  pl.BlockSpec((1, t, group * hd), lambda i, h, j: (i, 0, h)),
                  pl.BlockSpec((1, t, group * hd), lambda i, h, j: (i, 0, h))],
        out_specs=pl.BlockSpec((1, blk, group * hd), lambda i, h, j: (i, j, h)),
        out_shape=jax.ShapeDtypeStruct((b, t, width), BF16),
        scratch_shapes=[pltpu.VMEM((group, nbp, hd), F32),
                        pltpu.VMEM((group, t, 2 * hd), BF16),
                        pltpu.VMEM((group, 2 * hd, t), BF16),
                        pltpu.VMEM((group, t, 2 * hd), BF16),
                        pltpu.VMEM((group, LANES, blk), F32)],
        compiler_params=_cparams("parallel", "parallel", "arbitrary"),
        name="moba_prompt",
    )(q, k, v)


PAGES_PER_STEP = 16


def _block_sum_kernel(pt_ref, *refs, page, blk):
    del pt_ref
    k_refs, o_ref = refs[:-1], refs[-1]
    per_blk = blk // page
    for n in range(len(k_refs) // per_blk):
        acc = jnp.sum(k_refs[n * per_blk][...], axis=0)
        for r in range(1, per_blk):
            acc = acc + jnp.sum(k_refs[n * per_blk + r][...], axis=0)
        o_ref[0, n] = acc


def _paged_block_sums(cache, layer, page_table):
    _, _, page, heads, hd = cache.shape
    b, n_pages = page_table.shape
    pps = PAGES_PER_STEP
    assert MOBA_BLOCK % page == 0 and (pps * page) % MOBA_BLOCK == 0 and n_pages % pps == 0
    blocks_per_step = pps * page // MOBA_BLOCK

    def page_spec(r):
        return pl.BlockSpec((None, None, page, heads, hd),
                            lambda i, s, pt: (layer, pt[i, s * pps + r], 0, 0, 0))

    return pl.pallas_call(
        functools.partial(_block_sum_kernel, page=page, blk=MOBA_BLOCK),
        grid_spec=pltpu.PrefetchScalarGridSpec(
            num_scalar_prefetch=1,
            grid=(b, n_pages // pps),
            in_specs=[page_spec(r) for r in range(pps)],
            out_specs=pl.BlockSpec((1, blocks_per_step, heads, hd), lambda i, s, pt: (i, s, 0, 0)),
        ),
        out_shape=jax.ShapeDtypeStruct((b, n_pages * page // MOBA_BLOCK, heads, hd), F32),
        compiler_params=_cparams("parallel", "arbitrary"),
        name="paged_block_sums",
    )(page_table, *([cache] * pps))


def _decode_select_kernel(q_ref, ks_ref, o_ref, *, nb):
    gate = jnp.sum(ks_ref[0] * q_ref[0][None], axis=-1, keepdims=True) / MOBA_BLOCK
    idx = lax.broadcasted_iota(jnp.int32, gate.shape, 0)
    for r in range(MOBA_TOPK):
        m = jnp.max(gate, axis=0, keepdims=True)
        first = jnp.min(jnp.where(gate == m, idx, nb), axis=0, keepdims=True)
        o_ref[0, r] = jnp.broadcast_to(first[0], o_ref.shape[2:])
        gate = jnp.where(idx == first, NEG_INF, gate)


def _decode_select(q, ksum):
    b, nb, heads, hd = ksum.shape
    assert nb >= MOBA_TOPK
    out = pl.pallas_call(
        functools.partial(_decode_select_kernel, nb=nb),
        grid=(b,),
        in_specs=[pl.BlockSpec((1, heads, hd), lambda i: (i, 0, 0)),
                  pl.BlockSpec((1, nb, heads, hd), lambda i: (i, 0, 0, 0))],
        out_specs=pl.BlockSpec((1, MOBA_TOPK, heads, LANES), lambda i: (i, 0, 0, 0)),
        out_shape=jax.ShapeDtypeStruct((b, MOBA_TOPK, heads, LANES), jnp.int32),
        compiler_params=_cparams("parallel"),
        name="decode_select",
    )(q, ksum)
    return out[..., 0]


def _decode_attn_kernel(pt_ref, sel_ref, q_ref, kn_ref, vn_ref, *refs,
                        page, heads, past_len, per_blk):
    del pt_ref
    n_parts = MOBA_TOPK * per_blk
    k_refs, v_refs, o_ref = refs[:n_parts], refs[n_parts:2 * n_parts], refs[2 * n_parts]
    i, h = pl.program_id(0), pl.program_id(1)
    hd = q_ref.shape[2]
    q = q_ref[0, pl.ds(h, 1), :]
    scale = hd ** -0.5
    slope = jnp.exp2(-jnp.full((1, 1), h + 1, jnp.int32).astype(F32) * (8.0 / heads))

    m = jnp.sum(q * kn_ref[0, pl.ds(h, 1), :], axis=-1, keepdims=True) * scale
    l = jnp.ones_like(m)
    acc = vn_ref[0, pl.ds(h, 1), :]
    r = lax.broadcasted_iota(jnp.int32, (page, 1), 0)
    for j in range(MOBA_TOPK):
        n = sel_ref[(i * MOBA_TOPK + j) * heads + h]
        for part in range(per_blk):
            k_h = k_refs[j * per_blk + part][pl.ds(h, page, stride=heads), :]
            v_h = v_refs[j * per_blk + part][pl.ds(h, page, stride=heads), :]
            kpos = n * MOBA_BLOCK + part * page + r
            s = jnp.sum(k_h * q, axis=-1, keepdims=True) * scale - slope * (past_len - kpos).astype(F32)
            m_new = jnp.maximum(m, jnp.max(s, axis=0, keepdims=True))
            alpha = jnp.exp(m - m_new)
            p = jnp.exp(s - m_new)
            l = alpha * l + jnp.sum(p, axis=0, keepdims=True)
            acc = alpha * acc + jnp.sum(p * v_h, axis=0, keepdims=True)
            m = m_new
    o_ref[0, pl.ds(h, 1), :] = acc / l


def _moba_decode(q, k_new, v_new, cache_k, cache_v, layer, page_table, sel):
    b, heads, hd = q.shape
    n_layers, n_pool, page = cache_k.shape[:3]
    per_blk = MOBA_BLOCK // page
    past_len = page_table.shape[1] * page
    assert past_len % MOBA_BLOCK == 0
    rows = page * heads
    cache_k = cache_k.reshape(n_layers, n_pool, rows, hd)
    cache_v = cache_v.reshape(n_layers, n_pool, rows, hd)

    def page_spec(j, part):
        def index(i, h, pt, sl):
            n = sl[(i * MOBA_TOPK + j) * heads + h]
            return (layer, pt[i, n * per_blk + part], 0, 0)
        return pl.BlockSpec((None, None, rows, hd), index)

    tok_spec = pl.BlockSpec((1, heads, hd), lambda i, h, pt, sl: (i, 0, 0))
    page_specs = [page_spec(j, r) for j in range(MOBA_TOPK) for r in range(per_blk)]
    return pl.pallas_call(
        functools.partial(_decode_attn_kernel, page=page, heads=heads, past_len=past_len, per_blk=per_blk),
        grid_spec=pltpu.PrefetchScalarGridSpec(
            num_scalar_prefetch=2,
            grid=(b, heads),
            in_specs=[tok_spec, tok_spec, tok_spec] + page_specs * 2,
            out_specs=tok_spec,
        ),
        out_shape=jax.ShapeDtypeStruct((b, heads, hd), F32),
        compiler_params=_cparams("parallel", "arbitrary"),
        name="moba_decode",
    )(page_table, sel.reshape(-1), q, k_new, v_new,
      *([cache_k] * len(page_specs)), *([cache_v] * len(page_specs)))


def _gla_gate_kernel(x_ref, g_ref, wa_ref, w2_ref, b_ref, o_ref):
    hn = _rmsnorm(x_ref[...], g_ref[...])
    a = _wdot(hn, wa_ref[...], w_rows=True)
    z = _wdot(a, w2_ref[...]) + b_ref[...]
    o_ref[...] = jax.nn.log_sigmoid(z) / GLA_GATE_TAU


def _gla_gate(x, g, wa, w2, bias, layer, wlayer, *, tm=512):
    m, d = x.shape
    rank, n = w2.shape[1:]
    tm = _tile(m, tm)
    return pl.pallas_call(
        _gla_gate_kernel,
        grid=(m // tm,),
        in_specs=[pl.BlockSpec((tm, d), lambda i: (i, 0)),
                  _layer_spec((1, d), layer, lambda i: (0, 0)),
                  _layer_spec((rank, d), wlayer, lambda i: (0, 0)),
                  _layer_spec((rank, n), wlayer, lambda i: (0, 0)),
                  _layer_spec((1, n), wlayer, lambda i: (0, 0))],
        out_specs=pl.BlockSpec((tm, n), lambda i: (i, 0)),
        out_shape=jax.ShapeDtypeStruct((m, n), F32),
        compiler_params=_cparams("parallel"),
        name="gla_gate",
    )(x, _gain(g), wa, w2, _gain(bias))


def _gla_prompt_kernel(z_ref, lg_ref, gg_ref, o_ref, s_ref, *, heads, dk, dv):
    c = GLA_CHUNK

    @pl.when(pl.program_id(0) == 0)
    def _():
        s_ref[...] = jnp.zeros_like(s_ref)

    row = lax.broadcasted_iota(jnp.int32, (c, c), 0)
    col = lax.broadcasted_iota(jnp.int32, (c, c), 1)
    causal = col <= row
    tri = jnp.where(causal, 1.0, 0.0).astype(BF16)
    k0, v0, r0 = heads * dk, 2 * heads * dk, 2 * heads * dk + heads * dv
    streams = [(bi, h) for bi in range(z_ref.shape[0]) for h in range(heads)]

    cums = []
    for bi, h in streams:
        lg_hi, lg_mid, lg_lo = _split3(lg_ref[bi, :, h * dk:(h + 1) * dk])
        cums.append(_dot(tri, lg_hi) + (_dot(tri, lg_mid) + _dot(tri, lg_lo)))
    decs = []
    for (bi, h), b in zip(streams, cums):
        q = z_ref[bi, :, h * dk:(h + 1) * dk] * dk ** -0.5
        k = z_ref[bi, :, k0 + h * dk:k0 + (h + 1) * dk]
        b_last = b[c - 1:c, :]
        decs.append(((q * jnp.exp(b)).astype(BF16), (k * jnp.exp(-b)).astype(BF16),
                     (k * jnp.exp(b_last - b)).astype(BF16), b_last))
    attn = [jnp.where(causal, _dot_nt(q_dec, k_dec), 0.0).astype(BF16) for q_dec, k_dec, _, _ in decs]
    outs = []
    for (bi, h), a, (q_dec, _, k_rem, b_last) in zip(streams, attn, decs):
        v = z_ref[bi, :, v0 + h * dv:v0 + (h + 1) * dv].astype(BF16)
        state = s_ref[bi, h]
        outs.append(_dot(a, v) + _dot(q_dec, state.astype(BF16)))
        decay = jnp.exp(jnp.broadcast_to(b_last, (LANES, dk))).T[:, 0:1]
        s_ref[bi, h] = decay * state + _dot_tn(k_rem, v)
    for (bi, h), o in zip(streams, outs):
        r = z_ref[bi, :, r0 + h * dv:r0 + (h + 1) * dv]
        o_ref[bi, :, h * dv:(h + 1) * dv] = (
            _rmsnorm(o, gg_ref[...]) * (r * jax.nn.sigmoid(r))).astype(o_ref.dtype)


def _gla_prompt(z, lg, g_gla, layer, *, heads, dk, dv):
    b, t, zw = z.shape
    c = GLA_CHUNK
    assert t % c == 0 and zw == 2 * heads * (dk + dv)
    return pl.pallas_call(
        functools.partial(_gla_prompt_kernel, heads=heads, dk=dk, dv=dv),
        grid=(t // c,),
        in_specs=[pl.BlockSpec((b, c, zw), lambda j: (0, j, 0)),
                  pl.BlockSpec((b, c, heads * dk), lambda j: (0, j, 0)),
                  _layer_spec((1, dv), layer, lambda j: (0, 0))],
        out_specs=[pl.BlockSpec((b, c, heads * dv), lambda j: (0, j, 0)),
                   pl.BlockSpec((b, heads, dk, dv), lambda j: (0, 0, 0, 0))],
        out_shape=[jax.ShapeDtypeStruct((b, t, heads * dv), BF16),
                   jax.ShapeDtypeStruct((b, heads, dk, dv), F32)],
        compiler_params=_cparams("arbitrary"),
        name="gla_prompt",
    )(z, lg, _gain(g_gla))


def _gla_decode_kernel(q_ref, k_ref, lg_ref, v_ref, r_ref, gg_ref, s0_ref, o_ref, s_ref, *, dk):
    q = q_ref[0, 0] * dk ** -0.5
    k = k_ref[0, 0]
    b = lg_ref[0, 0]
    v = v_ref[0, 0]
    state = s0_ref[0, 0]
    q_dec = q * jnp.exp(b)
    a = jnp.sum(q_dec * (k * jnp.exp(-b)), axis=0, keepdims=True)
    o = a * v + jnp.sum(q_dec * state, axis=0, keepdims=True)
    s_ref[0, 0] = jnp.exp(b) * state + (k * jnp.exp(b - b)) * v
    r = r_ref[0, 0]
    o_ref[0, 0] = _rmsnorm(o, gg_ref[...]) * (r * jax.nn.sigmoid(r))


def _gla_decode(q, k, lg, v, r, g_gla, s0, layer):
    _, b, heads, dk, dv = s0.shape
    col = pl.BlockSpec((1, 1, dk, 1), lambda i, h: (i, h, 0, 0))
    rowspec = pl.BlockSpec((1, 1, 1, dv), lambda i, h: (i, h, 0, 0))
    st = pl.BlockSpec((1, 1, dk, dv), lambda i, h: (i, h, 0, 0))
    return pl.pallas_call(
        functools.partial(_gla_decode_kernel, dk=dk),
        grid=(b, heads),
        in_specs=[col, col, col, rowspec, rowspec,
                  _layer_spec((1, dv), layer, lambda i, h: (0, 0)),
                  _layer_spec((1, 1, dk, dv), layer, lambda i, h: (i, h, 0, 0))],
        out_specs=[rowspec, st],
        out_shape=[jax.ShapeDtypeStruct((b, heads, 1, dv), F32),
                   jax.ShapeDtypeStruct((b, heads, dk, dv), s0.dtype)],
        compiler_params=_cparams("parallel", "parallel"),
        name="gla_decode",
    )(q, k, lg, v, r, _gain(g_gla), s0)


def _trunk(x, p, start, pool_state, gla_state, cache_k, cache_v, page_table, wts):
    b, t, d = x.shape
    m = b * t
    depth = wts["g_mix"].shape[0]
    pool_w = wts["pool_scale"].shape[1]
    pool_buf = max(POOL_WINDOWS) - 1
    heads_m, hd = cache_k.shape[3], cache_k.shape[4]
    moba_w = heads_m * hd
    _, _, heads_g, dk, dv = gla_state.shape
    dk_tot, dv_tot = heads_g * dk, heads_g * dv
    decode = page_table is not None
    assert pool_w == moba_w
    p = p.reshape(depth, m, -1)

    h = x.reshape(m, d)
    pools, ks, vs, glas, ffn_bf16 = [], [], [], [], []
    for i in range(depth):
        j = i // 2
        if i % 2 == 0:
            u, q, k, v = (a.reshape(b, t, -1) for a in _norm_matmul(
                h, wts["g_mix"], wts["w_in_even"], i, j, pool_w + 3 * moba_w, n_out=4, tm=512, tn=1024))
            if decode:
                buf = pool_state[j]
                pools.append(jnp.concatenate([buf, u], axis=1)[:, -pool_buf:])
            else:
                buf = jnp.zeros((b, pool_buf, pool_w), x.dtype)
                pools.append(u[:, -pool_buf:])
            y_a = _pool_mixer(u, buf, start, wts["w_pool"], wts["pool_scale"], j)
            if decode:
                q3, k3, v3 = (a.reshape(b, heads_m, hd) for a in (q, k, v))
                ksum = _paged_block_sums(cache_k, j, page_table)
                sel = _decode_select(q3, ksum)
                y_b = _moba_decode(q3, k3, v3, cache_k, cache_v, j, page_table, sel)
            else:
                y_b = _moba_prompt(q, k, v)
            h = _proj_res([y_a.reshape(m, pool_w), y_b.reshape(m, moba_w)], wts["w_out_even"], j, h)
            ks.append(k.reshape(b, t, heads_m, hd))
            vs.append(v.reshape(b, t, heads_m, hd))
        else:
            n_main = 2 * dk_tot + 2 * dv_tot
            z = _norm_matmul(h, wts["g_mix"], wts["w_in_odd"], i, j, n_main, w_rows=True, tn=1024)
            lg = _gla_gate(h, wts["g_mix"], wts["w_gate_odd"], wts["w_gk2"], wts["b_gk"], i, j)
            if decode:
                def cols(a):
                    return a.reshape(b, heads_g, dk, 1)
                def rows(a):
                    return a.reshape(b, heads_g, 1, dv)
                og, s_new = _gla_decode(
                    cols(z[:, :dk_tot]), cols(z[:, dk_tot:2 * dk_tot]), cols(lg),
                    rows(z[:, 2 * dk_tot:2 * dk_tot + dv_tot]), rows(z[:, 2 * dk_tot + dv_tot:]),
                    wts["g_gla"], gla_state, j)
            else:
                og, s_new = _gla_prompt(z.reshape(b, t, n_main), lg.reshape(b, t, dk_tot), wts["g_gla"], j,
                                        heads=heads_g, dk=dk, dv=dv)
            h = _proj_res([og.reshape(m, dv_tot)], wts["w_out_odd"], j, h)
            glas.append(s_new.astype(gla_state.dtype))
        if decode:
            h, w16 = _ffn(h, wts["g_ffn"], (wts["w_ffn_gate"], wts["w_ffn_up"], wts["w_ffn_down"]), i,
                          emit_bf16=True)
            ffn_bf16.append(w16)
        else:
            h = _ffn(h, wts["g_ffn"], wts["ffn_bf16"][i], i)
        h = _ple(h, wts["g_ple"], wts["w_ple_gate"], p, wts["w_ple_proj"], i)
    y = _final_norm(h, wts["g_final"]).reshape(b, t, d)
    return (y, jnp.stack(pools), jnp.stack(ks), jnp.stack(vs), jnp.stack(glas)), ffn_bf16


_MATMUL_WEIGHTS = ("w_in_even", "w_pool", "w_out_even", "w_in_odd", "w_gk2", "w_out_odd",
                   "w_ple_gate", "w_ple_proj")


def kernel(x_prompt, x_sample, state_pool, cache_k, cache_v, state_gla, page_table, p_prompt, p_sample,
           g_mix, g_ffn, g_ple, g_final, w_in_even, w_pool, pool_scale, w_out_even,
           w_in_odd, w_gk2, b_gk, g_gla, w_out_odd, w_ffn_gate, w_ffn_up, w_ffn_down,
           w_ple_gate, w_ple_proj):
    wts = dict(g_mix=g_mix, g_ffn=g_ffn, g_ple=g_ple, g_final=g_final, w_in_even=w_in_even,
               w_pool=w_pool, pool_scale=pool_scale, w_out_even=w_out_even, w_in_odd=w_in_odd,
               w_gk2=w_gk2, b_gk=b_gk, g_gla=g_gla, w_out_odd=w_out_odd, w_ffn_gate=w_ffn_gate,
               w_ffn_up=w_ffn_up, w_ffn_down=w_ffn_down, w_ple_gate=w_ple_gate, w_ple_proj=w_ple_proj)
    n_main = 2 * state_gla.shape[2] * (state_gla.shape[3] + state_gla.shape[4])
    wts["w_in_odd"] = jnp.swapaxes(w_in_odd, 1, 2)
    wts["w_gate_odd"] = wts["w_in_odd"][:, n_main:, :]
    past_len = page_table.shape[1] * cache_k.shape[2]
    (y_s, pool_s, k_s, v_s, gla_s), ffn_bf16 = _trunk(x_sample, p_sample, past_len, state_pool, state_gla,
                                                      cache_k, cache_v, page_table, wts)
    wts_bf16 = dict(wts, ffn_bf16=ffn_bf16)
    for name in _MATMUL_WEIGHTS + ("w_gate_odd",):
        wts_bf16[name] = wts[name].astype(BF16)
    gla0 = jax.ShapeDtypeStruct((state_gla.shape[0], x_prompt.shape[0]) + state_gla.shape[2:], state_gla.dtype)
    (y_p, pool_p, k_p, v_p, gla_p), _ = _trunk(x_prompt, p_prompt, 0, None, gla0, cache_k, cache_v, None, wts_bf16)
    return (y_p, y_s, pool_p, pool_s, k_p, k_s, v_p, v_s, gla_p, gla_s)
```
